```python
import jax, jax.numpy as jnp
from jax import lax
import numpy as np

D_MODEL = 2048
BATCH = 1
SEQ = 8192
DEPTH = 2

GRID_W = 64
HEAD_DIM = 128
D_MIX = D_MODEL
N_HEADS_NA = (D_MIX // 2) // HEAD_DIM
N_HEADS_Q = (D_MIX // 2) // HEAD_DIM
N_KV_HEADS = 2
D_NA = N_HEADS_NA * HEAD_DIM
D_GQA = N_HEADS_Q * HEAD_DIM
D_KV = N_KV_HEADS * HEAD_DIM
D_IN = 3 * D_NA + D_GQA + 2 * D_KV
NA_KH = 8
NA_KW = 16
Q_BLOCK = 128
ROPE_THETA = 10000.0
D_FF = 5632
RMS_EPS = 1e-6
LN_EPS = 1e-5
DEEPNORM_ALPHA = (2.0 * DEPTH) ** 0.25
DEEPNORM_BETA = (8.0 * DEPTH) ** -0.25

kernel_name = "hybrid_natten_gqa_macaron_deepnorm"


def layer_norm(x, g, b):
    xf = x.astype(jnp.float32)
    mu = jnp.mean(xf, axis=-1, keepdims=True)
    xc = xf - mu
    var = jnp.mean(xc * xc, axis=-1, keepdims=True)
    y = xc * lax.rsqrt(var + LN_EPS)
    return (y * g.astype(jnp.float32) + b.astype(jnp.float32)).astype(x.dtype)


def rms_norm(x, g):
    xf = x.astype(jnp.float32)
    y = xf * lax.rsqrt(jnp.mean(xf * xf, axis=-1, keepdims=True) + RMS_EPS)
    return (y * g.astype(jnp.float32)).astype(x.dtype)


def swiglu(x, w_gate_up, w_down):
    gate, up = jnp.split(x @ w_gate_up, 2, axis=-1)
    return (jax.nn.silu(gate) * up) @ w_down


def axial_rope(x):
    _, S, _, D = x.shape
    half = D // 2
    nfreq = half // 2
    t = jnp.arange(S)
    row = (t // GRID_W).astype(jnp.float32)
    col = (t % GRID_W).astype(jnp.float32)
    inv_freq = 1.0 / (ROPE_THETA ** (jnp.arange(nfreq, dtype=jnp.float32) / nfreq))

    def rot(xh, pos):
        ang = pos[:, None] * inv_freq[None, :]
        cos = jnp.cos(ang)[None, :, None, :]
        sin = jnp.sin(ang)[None, :, None, :]
        x1, x2 = xh[..., :nfreq], xh[..., nfreq:]
        return jnp.concatenate([x1 * cos - x2 * sin, x2 * cos + x1 * sin], axis=-1)

    xf = x.astype(jnp.float32)
    out = jnp.concatenate([rot(xf[..., :half], row), rot(xf[..., half:], col)], axis=-1)
    return out.astype(x.dtype)


def neighbourhood_attention(q, k, v, rel_bias):
    B, S, H, D = q.shape
    rows = S // GRID_W
    kh = min(NA_KH, rows)
    r = jnp.arange(rows)
    r0 = jnp.clip(r - kh // 2, 0, rows - kh)
    row_idx = r0[:, None] + jnp.arange(kh)[None, :]
    c = jnp.arange(GRID_W)
    c0 = jnp.clip(c - NA_KW // 2, 0, GRID_W - NA_KW)
    kc = jnp.arange(GRID_W)
    col_in = (kc[None, :] >= c0[:, None]) & (kc[None, :] < c0[:, None] + NA_KW)
    drow_idx = row_idx - r[:, None] + (NA_KH - 1)
    dcol_idx = jnp.clip(kc[None, :] - c[:, None], -(NA_KW - 1), NA_KW - 1) + (NA_KW - 1)
    bias = rel_bias[:, drow_idx[:, :, None, None], dcol_idx[None, None, :, :]]
    bias = jnp.transpose(bias, (0, 1, 3, 2, 4)).astype(jnp.float32)

    scale = HEAD_DIM ** -0.5
    qg = q.reshape(B, rows, GRID_W, H, D)
    kg = k.reshape(B, rows, GRID_W, H, D)[:, row_idx]
    vg = v.reshape(B, rows, GRID_W, H, D)[:, row_idx]
    s = jnp.einsum('brchd,brukhd->bhrcuk', qg, kg).astype(jnp.float32) * scale
    s = s + bias[None]
    s = jnp.where(col_in[None, None, None, :, None, :], s, -jnp.inf)
    sh = s.shape
    p = jax.nn.softmax(s.reshape(sh[:4] + (kh * GRID_W,)), axis=-1).reshape(sh)
    o = jnp.einsum('bhrcuk,brukhd->brchd', p.astype(v.dtype), vg)
    return o.reshape(B, S, H, D)


def gqa_block_attention(q, k, v):
    B, S, Hq, D = q.shape
    Hkv = k.shape[2]
    G = Hq // Hkv
    nblk = S // Q_BLOCK
    scale = HEAD_DIM ** -0.5
    qb = jnp.transpose(q.reshape(B, nblk, Q_BLOCK, Hkv, G, D), (1, 0, 2, 3, 4, 5))

    def one_block(qblk):
        s = jnp.einsum('bqhgd,bkhd->bhgqk', qblk, k).astype(jnp.float32) * scale
        p = jax.nn.softmax(s, axis=-1)
        return jnp.einsum('bhgqk,bkhd->bqhgd', p.astype(v.dtype), v)

    o = lax.map(one_block, qb)
    return jnp.transpose(o, (1, 0, 2, 3, 4, 5)).reshape(B, S, Hq * D)


def setup_inputs(seed: int = 0) -> dict:
    key = jax.random.key(seed)
    ks = jax.random.split(key, 20)
    f32 = jnp.float32

    def nrm(k, shape, scale):
        return jax.random.normal(k, shape, f32) * scale

    def gain(k, shape):
        return 1.0 + 0.05 * jax.random.normal(k, shape, f32)

    return {
        "x": jax.random.normal(ks[0], (BATCH, SEQ, D_MODEL), f32),
        "ffn1_w_gate_up": nrm(ks[1], (DEPTH, D_MODEL, 2 * D_FF), D_MODEL ** -0.5),
        "ffn1_w_down": nrm(ks[2], (DEPTH, D_FF, D_MODEL), DEEPNORM_BETA * D_FF ** -0.5),
        "ln1_g": gain(ks[3], (DEPTH, D_MODEL)),
        "ln1_b": nrm(ks[4], (DEPTH, D_MODEL), 0.02),
        "w_in": nrm(ks[5], (DEPTH, D_MODEL, D_IN), D_MODEL ** -0.5),
        "na_rel_bias": nrm(ks[6], (DEPTH, N_HEADS_NA, 2 * NA_KH - 1, 2 * NA_KW - 1), 0.1),
        "q_norm_g": gain(ks[7], (DEPTH, HEAD_DIM)),
        "k_norm_g": gain(ks[8], (DEPTH, HEAD_DIM)),
        "gn_na_g": gain(ks[9], (DEPTH, D_NA)),
        "gn_gqa_g": gain(ks[10], (DEPTH, D_GQA)),
        "w_out": nrm(ks[11], (DEPTH, D_MIX, D_MODEL), DEEPNORM_BETA * D_MIX ** -0.5),
        "ln2_g": gain(ks[12], (DEPTH, D_MODEL)),
        "ln2_b": nrm(ks[13], (DEPTH, D_MODEL), 0.02),
        "ffn2_w_gate_up": nrm(ks[14], (DEPTH, D_MODEL, 2 * D_FF), D_MODEL ** -0.5),
        "ffn2_w_down": nrm(ks[15], (DEPTH, D_FF, D_MODEL), DEEPNORM_BETA * D_FF ** -0.5),
        "ln3_g": gain(ks[16], (DEPTH, D_MODEL)),
        "ln3_b": nrm(ks[17], (DEPTH, D_MODEL), 0.02),
    }


def reference(x, ffn1_w_gate_up, ffn1_w_down, ln1_g, ln1_b, w_in, na_rel_bias,
              q_norm_g, k_norm_g, gn_na_g, gn_gqa_g, w_out, ln2_g, ln2_b,
              ffn2_w_gate_up, ffn2_w_down, ln3_g, ln3_b):
    B, S, _ = x.shape
    for l in range(DEPTH):
        x = layer_norm(DEEPNORM_ALPHA * x + 0.5 * swiglu(x, ffn1_w_gate_up[l], ffn1_w_down[l]),
                       ln1_g[l], ln1_b[l])

        h = x @ w_in[l]
        q_na, k_na, v_na, q_g, k_g, v_g = jnp.split(
            h, np.cumsum([D_NA, D_NA, D_NA, D_GQA, D_KV]).tolist(), axis=-1)

        o_na = neighbourhood_attention(
            q_na.reshape(B, S, N_HEADS_NA, HEAD_DIM),
            k_na.reshape(B, S, N_HEADS_NA, HEAD_DIM),
            v_na.reshape(B, S, N_HEADS_NA, HEAD_DIM),
            na_rel_bias[l]).reshape(B, S, D_NA)

        qh = axial_rope(rms_norm(q_g.reshape(B, S, N_HEADS_Q, HEAD_DIM), q_norm_g[l]))
        kh = axial_rope(rms_norm(k_g.reshape(B, S, N_KV_HEADS, HEAD_DIM), k_norm_g[l]))
        vh = v_g.reshape(B, S, N_KV_HEADS, HEAD_DIM)
        o_g = gqa_block_attention(qh, kh, vh)

        mix = jnp.concatenate([rms_norm(o_na, gn_na_g[l]), rms_norm(o_g, gn_gqa_g[l])], axis=-1) @ w_out[l]
        x = layer_norm(DEEPNORM_ALPHA * x + mix, ln2_g[l], ln2_b[l])

        x = layer_norm(DEEPNORM_ALPHA * x + 0.5 * swiglu(x, ffn2_w_gate_up[l], ffn2_w_down[l]),
                       ln3_g[l], ln3_b[l])
    return x
```

```python
import functools

import jax
import jax.numpy as jnp
from jax import lax
from jax.experimental import pallas as pl
from jax.experimental.pallas import tpu as pltpu

F32 = jnp.float32
BF16 = jnp.bfloat16

GRID_W = 64
HEAD_DIM = 128
N_KV_HEADS = 2
NA_KH = 8
NA_KW = 16
ROPE_THETA = 10000.0
RMS_EPS = 1e-6
LN_EPS = 1e-5
SM_SCALE = HEAD_DIM ** -0.5
MASK_VALUE = -1e30

FFN_TM = 512
FFN_TF = 512
PROJ_TM = 512
PROJ_TN = 512
GQA_TQ = 256
GQA_TK = 512
NA_ROWS = 8
VMEM_LIMIT = 56 * 1024 * 1024


def _layer_norm(y, g, b):
    mu = jnp.mean(y, axis=-1, keepdims=True)
    yc = y - mu
    var = jnp.mean(yc * yc, axis=-1, keepdims=True)
    return yc * lax.rsqrt(var + LN_EPS) * g + b


def _rms_scale(t):
    return lax.rsqrt(jnp.mean(t * t, axis=-1, keepdims=True) + RMS_EPS)


def _ffn_ln_kernel(x_ref, wg_ref, wu_ref, wd_ref, g_ref, b_ref, o_ref, xb_ref, acc_ref, *, alpha, nf):
    j = pl.program_id(1)

    @pl.when(j == 0)
    def _():
        xb_ref[...] = x_ref[...].astype(BF16)
        acc_ref[...] = jnp.zeros_like(acc_ref)

    xb = xb_ref[...]
    gate = jnp.dot(xb, wg_ref[...], preferred_element_type=F32)
    up = jnp.dot(xb, wu_ref[...], preferred_element_type=F32)
    hidden = (gate / (1.0 + jnp.exp(-gate)) * up).astype(BF16)
    acc_ref[...] += jnp.dot(hidden, wd_ref[...], preferred_element_type=F32)

    @pl.when(j == nf - 1)
    def _():
        y = alpha * x_ref[...] + 0.5 * acc_ref[...]
        o_ref[...] = _layer_norm(y, g_ref[...], b_ref[...])


def _ffn_ln(x, w_gu, w_d, ln_g, ln_b, layer, alpha):
    s, d = x.shape
    d_ff = w_d.shape[1]
    tm, tf = FFN_TM, FFN_TF
    assert s % tm == 0 and d_ff % tf == 0
    nf = d_ff // tf
    return pl.pallas_call(
        functools.partial(_ffn_ln_kernel, alpha=alpha, nf=nf),
        grid=(s // tm, nf),
        in_specs=[
            pl.BlockSpec((tm, d), lambda i, j: (i, 0)),
            pl.BlockSpec((None, d, tf), lambda i, j: (layer, 0, j)),
            pl.BlockSpec((None, d, tf), lambda i, j: (layer, 0, j + nf)),
            pl.BlockSpec((None, tf, d), lambda i, j: (layer, j, 0)),
            pl.BlockSpec((None, 1, d), lambda i, j: (layer, 0, 0)),
            pl.BlockSpec((None, 1, d), lambda i, j: (layer, 0, 0)),
        ],
        out_specs=pl.BlockSpec((tm, d), lambda i, j: (i, 0)),
        out_shape=jax.ShapeDtypeStruct((s, d), F32),
        scratch_shapes=[pltpu.VMEM((tm, d), BF16), pltpu.VMEM((tm, d), F32)],
        compiler_params=pltpu.CompilerParams(
            dimension_semantics=("parallel", "arbitrary"), vmem_limit_bytes=VMEM_LIMIT),
        name="ffn_ln",
    )(x, w_gu, w_gu, w_d, ln_g, ln_b)


def _rope(t, cos, sin_lo, sin_hi):
    return t * cos + pltpu.roll(t, 96, 1) * sin_lo + pltpu.roll(t, 32, 1) * sin_hi


def _in_proj_kernel(x_ref, w_ref, cos_ref, slo_ref, shi_ref, qg_ref, kg_ref, o_ref, xb_ref,
                    *, n_qna, n_plain, n_qg):
    j = pl.program_id(1)

    @pl.when(j == 0)
    def _():
        xb_ref[...] = x_ref[...].astype(BF16)

    acc = jnp.dot(xb_ref[...], w_ref[...], preferred_element_type=F32)
    heads_per_tile = acc.shape[1] // HEAD_DIM

    def normed_rope(t, gain):
        t = t * _rms_scale(t) * gain
        return _rope(t, cos_ref[...], slo_ref[...], shi_ref[...])

    @pl.when(j < n_qna)
    def _():
        o_ref[...] = (acc * SM_SCALE).astype(o_ref.dtype)

    @pl.when((j >= n_qna) & (j < n_plain))
    def _():
        o_ref[...] = acc.astype(o_ref.dtype)

    @pl.when((j >= n_plain) & (j < n_plain + n_qg))
    def _():
        for h in range(heads_per_tile):
            sl = slice(h * HEAD_DIM, (h + 1) * HEAD_DIM)
            o_ref[:, sl] = (normed_rope(acc[:, sl], qg_ref[...]) * SM_SCALE).astype(o_ref.dtype)

    @pl.when(j == n_plain + n_qg)
    def _():
        for h in range(heads_per_tile):
            sl = slice(h * HEAD_DIM, (h + 1) * HEAD_DIM)
            if h < N_KV_HEADS:
                o_ref[:, sl] = normed_rope(acc[:, sl], kg_ref[...]).astype(o_ref.dtype)
            else:
                o_ref[:, sl] = acc[:, sl].astype(o_ref.dtype)


def _in_proj(x, w_in, cos, sin_lo, sin_hi, q_gain, k_gain, layer):
    s, d = x.shape
    d_in = w_in.shape[2]
    tm, tn = PROJ_TM, PROJ_TN
    d_na = d // 2
    assert d_na % tn == 0 and 2 * N_KV_HEADS * HEAD_DIM == tn and d_in == 4 * d_na + tn
    n_qna = d_na // tn
    n_plain = 3 * d_na // tn
    n_qg = d_na // tn
    return pl.pallas_call(
        functools.partial(_in_proj_kernel, n_qna=n_qna, n_plain=n_plain, n_qg=n_qg),
        grid=(s // tm, d_in // tn),
        in_specs=[
            pl.BlockSpec((tm, d), lambda i, j: (i, 0)),
            pl.BlockSpec((None, d, tn), lambda i, j: (layer, 0, j)),
            pl.BlockSpec((tm, HEAD_DIM), lambda i, j: (i, 0)),
            pl.BlockSpec((tm, HEAD_DIM), lambda i, j: (i, 0)),
            pl.BlockSpec((tm, HEAD_DIM), lambda i, j: (i, 0)),
            pl.BlockSpec((None, 1, HEAD_DIM), lambda i, j: (layer, 0, 0)),
            pl.BlockSpec((None, 1, HEAD_DIM), lambda i, j: (layer, 0, 0)),
        ],
        out_specs=pl.BlockSpec((tm, tn), lambda i, j: (i, j)),
        out_shape=jax.ShapeDtypeStruct((s, d_in), BF16),
        scratch_shapes=[pltpu.VMEM((tm, d), BF16)],
        compiler_params=pltpu.CompilerParams(
            dimension_semantics=("parallel", "arbitrary"), vmem_limit_bytes=VMEM_LIMIT),
        name="in_proj",
    )(x, w_in, cos, sin_lo, sin_hi, q_gain, k_gain)


def _na_kernel(q_ref, k_ref, v_ref, bias_ref, o_ref, *, rows_per_step, n_rows, kh):
    rb = pl.program_id(1)
    for rr in range(rows_per_step):
        r = rb * rows_per_step + rr
        r0 = jnp.clip(r - kh // 2, 0, n_rows - kh)
        start = pl.multiple_of(r0 * GRID_W, GRID_W)
        q = q_ref[rr * GRID_W:(rr + 1) * GRID_W, :]
        k = k_ref[pl.ds(start, kh * GRID_W), :]
        v = v_ref[pl.ds(start, kh * GRID_W), :]
        s = lax.dot_general(q, k, (((1,), (1,)), ((), ())), preferred_element_type=F32)
        s = s + bias_ref[r - r0]
        m = jnp.max(s, axis=-1, keepdims=True)
        p = jnp.exp(s - m)
        l = jnp.sum(p, axis=-1, keepdims=True)
        o = jnp.dot(p.astype(BF16), v, preferred_element_type=F32) / l
        o_ref[rr * GRID_W:(rr + 1) * GRID_W, :] = o.astype(o_ref.dtype)


def _na_attn(h, bias, layer, n_heads):
    s = h.shape[0]
    n_rows = s // GRID_W
    kh = min(NA_KH, n_rows)
    rows_per_step = NA_ROWS
    assert n_rows % rows_per_step == 0
    tq = rows_per_step * GRID_W
    return pl.pallas_call(
        functools.partial(_na_kernel, rows_per_step=rows_per_step, n_rows=n_rows, kh=kh),
        grid=(n_heads, n_rows // rows_per_step),
        in_specs=[
            pl.BlockSpec((tq, HEAD_DIM), lambda hd, rb: (rb, hd)),
            pl.BlockSpec((s, HEAD_DIM), lambda hd, rb: (0, n_heads + hd)),
            pl.BlockSpec((s, HEAD_DIM), lambda hd, rb: (0, 2 * n_heads + hd)),
            pl.BlockSpec((None, None, kh, GRID_W, kh * GRID_W), lambda hd, rb: (layer, hd, 0, 0, 0)),
        ],
        out_specs=pl.BlockSpec((tq, HEAD_DIM), lambda hd, rb: (rb, hd)),
        out_shape=jax.ShapeDtypeStruct((s, n_heads * HEAD_DIM), BF16),
        compiler_params=pltpu.CompilerParams(
            dimension_semantics=("parallel", "arbitrary"), vmem_limit_bytes=VMEM_LIMIT),
        name="na_attn",
    )(h, h, h, bias)


def _na_bias_table(rel_bias, n_rows):
    kh = min(NA_KH, n_rows)
    c = jnp.arange(GRID_W)
    c0 = jnp.clip(c - NA_KW // 2, 0, GRID_W - NA_KW)
    col_in = (c[None, :] >= c0[:, None]) & (c[None, :] < c0[:, None] + NA_KW)
    dcol = jnp.clip(c[None, :] - c[:, None], -(NA_KW - 1), NA_KW - 1) + (NA_KW - 1)
    p = jnp.arange(kh)
    u = jnp.arange(kh)
    drow = u[None, :] - p[:, None] + (NA_KH - 1)
    tab = rel_bias[:, :, drow[:, :, None, None], dcol[None, None, :, :]]
    tab = jnp.where(col_in[None, None, None, None], tab.astype(F32), MASK_VALUE)
    tab = jnp.transpose(tab, (0, 1, 2, 4, 3, 5))
    return tab.reshape(tab.shape[0], tab.shape[1], kh, GRID_W, kh * GRID_W)


def _gqa_kernel(q_ref, k_ref, v_ref, o_ref, *, group, tk, nk):
    tq = q_ref.shape[0]
    q = q_ref[...]
    qs = jnp.concatenate([q[:, g * HEAD_DIM:(g + 1) * HEAD_DIM] for g in range(group)], axis=0)

    def body(c, carry):
        m, l, acc = carry
        start = pl.multiple_of(c * tk, tk)
        k = k_ref[pl.ds(start, tk), :]
        v = v_ref[pl.ds(start, tk), :]
        s = lax.dot_general(qs, k, (((1,), (1,)), ((), ())), preferred_element_type=F32)
        m_new = jnp.maximum(m, jnp.max(s, axis=-1, keepdims=True))
        corr = jnp.exp(m - m_new)
        p = jnp.exp(s - m_new)
        l = corr * l + jnp.sum(p, axis=-1, keepdims=True)
        acc = corr * acc + jnp.dot(p.astype(BF16), v, preferred_element_type=F32)
        return m_new, l, acc

    init = (jnp.full((group * tq, 1), -jnp.inf, F32), jnp.zeros((group * tq, 1), F32),
            jnp.zeros((group * tq, HEAD_DIM), F32))
    _, l, acc = lax.fori_loop(0, nk, body, init)
    o = acc / l
    for g in range(group):
        o_ref[:, g * HEAD_DIM:(g + 1) * HEAD_DIM] = o[g * tq:(g + 1) * tq].astype(o_ref.dtype)


def _gqa_attn(h, d_na):
    s = h.shape[0]
    d_gqa = d_na
    group = d_gqa // HEAD_DIM // N_KV_HEADS
    tq, tk = GQA_TQ, GQA_TK
    assert s % tq == 0 and s % tk == 0
    gw = group * HEAD_DIM
    q_blk0 = 3 * d_na // gw
    k_blk0 = (3 * d_na + d_gqa) // HEAD_DIM
    v_blk0 = k_blk0 + N_KV_HEADS
    return pl.pallas_call(
        functools.partial(_gqa_kernel, group=group, tk=tk, nk=s // tk),
        grid=(N_KV_HEADS, s // tq),
        in_specs=[
            pl.BlockSpec((tq, gw), lambda kv, i: (i, q_blk0 + kv)),
            pl.BlockSpec((s, HEAD_DIM), lambda kv, i: (0, k_blk0 + kv)),
            pl.BlockSpec((s, HEAD_DIM), lambda kv, i: (0, v_blk0 + kv)),
        ],
        out_specs=pl.BlockSpec((tq, gw), lambda kv, i: (i, kv)),
        out_shape=jax.ShapeDtypeStruct((s, d_gqa), BF16),
        compiler_params=pltpu.CompilerParams(
            dimension_semantics=("parallel", "arbitrary"), vmem_limit_bytes=VMEM_LIMIT),
        name="gqa_attn",
    )(h, h, h)


def _out_proj_ln_kernel(ona_ref, og_ref, x_ref, w_ref, gna_ref, gg_ref, g_ref, b_ref, o_ref, *, alpha):
    d_na = ona_ref.shape[1]
    a = ona_ref[...].astype(F32)
    a = (a * _rms_scale(a) * gna_ref[...]).astype(BF16)
    b = og_ref[...].astype(F32)
    b = (b * _rms_scale(b) * gg_ref[...]).astype(BF16)
    mix = jnp.dot(a, w_ref[:d_na, :], preferred_element_type=F32)
    mix = mix + jnp.dot(b, w_ref[d_na:, :], preferred_element_type=F32)
    y = alpha * x_ref[...] + mix
    o_ref[...] = _layer_norm(y, g_ref[...], b_ref[...])


def _out_proj_ln(o_na, o_g, x, w_out, gn_na, gn_g, ln_g, ln_b, layer, alpha):
    s, d = x.shape
    d_na = o_na.shape[1]
    d_g = o_g.shape[1]
    tm = PROJ_TM
    return pl.pallas_call(
        functools.partial(_out_proj_ln_kernel, alpha=alpha),
        grid=(s // tm,),
        in_specs=[
            pl.BlockSpec((tm, d_na), lambda i: (i, 0)),
            pl.BlockSpec((tm, d_g), lambda i: (i, 0)),
            pl.BlockSpec((tm, d), lambda i: (i, 0)),
            pl.BlockSpec((None, d_na + d_g, d), lambda i: (layer, 0, 0)),
            pl.BlockSpec((None, 1, d_na), lambda i: (layer, 0, 0)),
            pl.BlockSpec((None, 1, d_g), lambda i: (layer, 0, 0)),
            pl.BlockSpec((None, 1, d), lambda i: (layer, 0, 0)),
            pl.BlockSpec((None, 1, d), lambda i: (layer, 0, 0)),
        ],
        out_specs=pl.BlockSpec((tm, d), lambda i: (i, 0)),
        out_shape=jax.ShapeDtypeStruct((s, d), F32),
        compiler_params=pltpu.CompilerParams(
            dimension_semantics=("parallel",), vmem_limit_bytes=VMEM_LIMIT),
        name="out_proj_ln",
    )(o_na, o_g, x, w_out, gn_na, gn_g, ln_g, ln_b)


def _rope_tables(s):
    half = HEAD_DIM // 2
    nfreq = half // 2
    t = jnp.arange(s)
    row = (t // GRID_W).astype(F32)
    col = (t % GRID_W).astype(F32)
    inv_freq = 1.0 / (ROPE_THETA ** (jnp.arange(nfreq, dtype=F32) / nfreq))
    lane = jnp.arange(HEAD_DIM)
    pos = jnp.where(lane[None, :] < half, row[:, None], col[:, None])
    ang = pos * inv_freq[lane % nfreq][None, :]
    cos, sin = jnp.cos(ang), jnp.sin(ang)
    lower = (lane % half) < nfreq
    sin_lo = jnp.where(lower[None, :], -sin, 0.0)
    sin_hi = jnp.where(lower[None, :], 0.0, sin)
    return cos, sin_lo, sin_hi


def kernel(x, ffn1_w_gate_up, ffn1_w_down, ln1_g, ln1_b, w_in, na_rel_bias, q_norm_g, k_norm_g,
           gn_na_g, gn_gqa_g, w_out, ln2_g, ln2_b, ffn2_w_gate_up, ffn2_w_down, ln3_g, ln3_b):
    batch, s, d = x.shape
    depth = w_in.shape[0]
    alpha = (2.0 * depth) ** 0.25
    d_na = d // 2
    n_heads_na = d_na // HEAD_DIM

    def row(p):
        return p.astype(F32)[:, None, :]

    wgu1, wd1 = ffn1_w_gate_up.astype(BF16), ffn1_w_down.astype(BF16)
    wgu2, wd2 = ffn2_w_gate_up.astype(BF16), ffn2_w_down.astype(BF16)
    w_in_b, w_out_b = w_in.astype(BF16), w_out.astype(BF16)
    cos, sin_lo, sin_hi = _rope_tables(s)
    bias = _na_bias_table(na_rel_bias, s // GRID_W)
    ln1g, ln1b, ln2g, ln2b, ln3g, ln3b = map(row, (ln1_g, ln1_b, ln2_g, ln2_b, ln3_g, ln3_b))
    qg, kg, gna, ggq = map(row, (q_norm_g, k_norm_g, gn_na_g, gn_gqa_g))

    outs = []
    for bi in range(batch):
        xs = x[bi]
        for layer in range(depth):
            xs = _ffn_ln(xs, wgu1, wd1, ln1g, ln1b, layer, alpha)
            h = _in_proj(xs, w_in_b, cos, sin_lo, sin_hi, qg, kg, layer)
            o_na = _na_attn(h, bias, layer, n_heads_na)
            o_g = _gqa_attn(h, d_na)
            xs = _out_proj_ln(o_na, o_g, xs, w_out_b, gna, ggq, ln2g, ln2b, layer, alpha)
            xs = _ffn_ln(xs, wgu2, wd2, ln3g, ln3b, layer, alpha)
        outs.append(xs)
    return outs[0][None] if batch == 1 else jnp.stack(outs, axis=0)
```

```python
import functools

import jax
import jax.numpy as jnp
from jax import lax
from jax.experimental import pallas as pl
from jax.experimental.pallas import tpu as pltpu

F32 = jnp.float32
BF16 = jnp.bfloat16

GRID_W = 64
HEAD_DIM = 128
N_KV_HEADS = 2
NA_KH = 8
NA_KW = 16
ROPE_THETA = 10000.0
RMS_EPS = 1e-6
LN_EPS = 1e-5
LOG2E = 1.4426950408889634
SM_SCALE = HEAD_DIM ** -0.5 * LOG2E
MASK_VALUE = -1e30

FFN_TM = 512
FFN_TF = 512
PROJ_TM = 512
PROJ_TN = 512
GQA_TQ = 1024
GQA_TK = 512
NA_ROWS = 8
VMEM_LIMIT = 56 * 1024 * 1024


def _layer_norm(y, g, b):
    mu = jnp.mean(y, axis=-1, keepdims=True)
    yc = y - mu
    var = jnp.mean(yc * yc, axis=-1, keepdims=True)
    return yc * lax.rsqrt(var + LN_EPS) * g + b


def _rms_scale(t):
    return lax.rsqrt(jnp.mean(t * t, axis=-1, keepdims=True) + RMS_EPS)


def _ffn_ln_kernel(x_ref, wg_ref, wu_ref, wd_ref, g_ref, b_ref, o_ref, xb_ref, acc_ref, *, alpha, nf):
    j = pl.program_id(1)

    @pl.when(j == 0)
    def _():
        xb_ref[...] = x_ref[...].astype(BF16)
        acc_ref[...] = jnp.zeros_like(acc_ref)

    xb = xb_ref[...]
    gate = jnp.dot(xb, wg_ref[...], preferred_element_type=F32)
    up = jnp.dot(xb, wu_ref[...], preferred_element_type=F32)
    hidden = (gate / (1.0 + jnp.exp(-gate)) * up).astype(BF16)
    acc_ref[...] += jnp.dot(hidden, wd_ref[...], preferred_element_type=F32)

    @pl.when(j == nf - 1)
    def _():
        y = alpha * x_ref[...] + 0.5 * acc_ref[...]
        o_ref[...] = _layer_norm(y, g_ref[...], b_ref[...])


def _ffn_ln(x, w_gu, w_d, ln_g, ln_b, layer, alpha):
    s, d = x.shape
    d_ff = w_d.shape[1]
    tm, tf = FFN_TM, FFN_TF
    assert s % tm == 0 and d_ff % tf == 0
    nf = d_ff // tf
    return pl.pallas_call(
        functools.partial(_ffn_ln_kernel, alpha=alpha, nf=nf),
        grid=(s // tm, nf),
        in_specs=[
            pl.BlockSpec((tm, d), lambda i, j: (i, 0)),
            pl.BlockSpec((None, d, tf), lambda i, j: (layer, 0, j)),
            pl.BlockSpec((None, d, tf), lambda i, j: (layer, 0, j + nf)),
            pl.BlockSpec((None, tf, d), lambda i, j: (layer, j, 0)),
            pl.BlockSpec((None, 1, d), lambda i, j: (layer, 0, 0)),
            pl.BlockSpec((None, 1, d), lambda i, j: (layer, 0, 0)),
        ],
        out_specs=pl.BlockSpec((tm, d), lambda i, j: (i, 0)),
        out_shape=jax.ShapeDtypeStruct((s, d), F32),
        scratch_shapes=[pltpu.VMEM((tm, d), BF16), pltpu.VMEM((tm, d), F32)],
        compiler_params=pltpu.CompilerParams(
            dimension_semantics=("parallel", "arbitrary"), vmem_limit_bytes=VMEM_LIMIT),
        name="ffn_ln",
    )(x, w_gu, w_gu, w_d, ln_g, ln_b)


def _rope(t, cos, sin_lo, sin_hi):
    return t * cos + pltpu.roll(t, 96, 1) * sin_lo + pltpu.roll(t, 32, 1) * sin_hi


def _in_proj_kernel(x_ref, w_ref, cos_ref, slo_ref, shi_ref, qg_ref, kg_ref, o_ref, xb_ref,
                    *, n_qna, n_plain, n_qg):
    j = pl.program_id(1)

    @pl.when(j == 0)
    def _():
        xb_ref[...] = x_ref[...].astype(BF16)

    acc = jnp.dot(xb_ref[...], w_ref[...], preferred_element_type=F32)
    heads_per_tile = acc.shape[1] // HEAD_DIM

    def normed_rope(t, gain):
        t = t * _rms_scale(t) * gain
        return _rope(t, cos_ref[...], slo_ref[...], shi_ref[...])

    @pl.when(j < n_qna)
    def _():
        o_ref[...] = (acc * SM_SCALE).astype(o_ref.dtype)

    @pl.when((j >= n_qna) & (j < n_plain))
    def _():
        o_ref[...] = acc.astype(o_ref.dtype)

    @pl.when((j >= n_plain) & (j < n_plain + n_qg))
    def _():
        for h in range(heads_per_tile):
            sl = slice(h * HEAD_DIM, (h + 1) * HEAD_DIM)
            o_ref[:, sl] = (normed_rope(acc[:, sl], qg_ref[...]) * SM_SCALE).astype(o_ref.dtype)

    @pl.when(j == n_plain + n_qg)
    def _():
        for h in range(heads_per_tile):
            sl = slice(h * HEAD_DIM, (h + 1) * HEAD_DIM)
            if h < N_KV_HEADS:
                o_ref[:, sl] = normed_rope(acc[:, sl], kg_ref[...]).astype(o_ref.dtype)
            else:
                o_ref[:, sl] = acc[:, sl].astype(o_ref.dtype)


def _in_proj(x, w_in, cos, sin_lo, sin_hi, q_gain, k_gain, layer):
    s, d = x.shape
    d_in = w_in.shape[2]
    tm, tn = PROJ_TM, PROJ_TN
    d_na = d // 2
    assert d_na % tn == 0 and 2 * N_KV_HEADS * HEAD_DIM == tn and d_in == 4 * d_na + tn
    n_qna = d_na // tn
    n_plain = 3 * d_na // tn
    n_qg = d_na // tn
    return pl.pallas_call(
        functools.partial(_in_proj_kernel, n_qna=n_qna, n_plain=n_plain, n_qg=n_qg),
        grid=(s // tm, d_in // tn),
        in_specs=[
            pl.BlockSpec((tm, d), lambda i, j: (i, 0)),
            pl.BlockSpec((None, d, tn), lambda i, j: (layer, 0, j)),
            pl.BlockSpec((tm, HEAD_DIM), lambda i, j: (i, 0)),
            pl.BlockSpec((tm, HEAD_DIM), lambda i, j: (i, 0)),
            pl.BlockSpec((tm, HEAD_DIM), lambda i, j: (i, 0)),
            pl.BlockSpec((None, 1, HEAD_DIM), lambda i, j: (layer, 0, 0)),
            pl.BlockSpec((None, 1, HEAD_DIM), lambda i, j: (layer, 0, 0)),
        ],
        out_specs=pl.BlockSpec((tm, tn), lambda i, j: (i, j)),
        out_shape=jax.ShapeDtypeStruct((s, d_in), BF16),
        scratch_shapes=[pltpu.VMEM((tm, d), BF16)],
        compiler_params=pltpu.CompilerParams(
            dimension_semantics=("parallel", "arbitrary"), vmem_limit_bytes=VMEM_LIMIT),
        name="in_proj",
    )(x, w_in, cos, sin_lo, sin_hi, q_gain, k_gain)


def _na_kernel(q_ref, k_ref, v_ref, bias_ref, o_ref, *, rows_per_step, n_rows, kh):
    rb = pl.program_id(1)
    for rr in range(rows_per_step):
        r = rb * rows_per_step + rr
        r0 = jnp.clip(r - kh // 2, 0, n_rows - kh)
        start = pl.multiple_of(r0 * GRID_W, GRID_W)
        q = q_ref[rr * GRID_W:(rr + 1) * GRID_W, :]
        k = k_ref[pl.ds(start, kh * GRID_W), :]
        v = v_ref[pl.ds(start, kh * GRID_W), :]
        s = lax.dot_general(q, k, (((1,), (1,)), ((), ())), preferred_element_type=F32)
        s = s + bias_ref[r - r0]
        m = jnp.max(s, axis=-1, keepdims=True)
        p = jnp.exp2(s - m)
        l = jnp.sum(p, axis=-1, keepdims=True)
        o = jnp.dot(p.astype(BF16), v, preferred_element_type=F32) / l
        o_ref[rr * GRID_W:(rr + 1) * GRID_W, :] = o.astype(o_ref.dtype)


def _na_attn(h, bias, layer, n_heads):
    s = h.shape[0]
    n_rows = s // GRID_W
    kh = min(NA_KH, n_rows)
    rows_per_step = NA_ROWS
    assert n_rows % rows_per_step == 0
    tq = rows_per_step * GRID_W
    return pl.pallas_call(
        functools.partial(_na_kernel, rows_per_step=rows_per_step, n_rows=n_rows, kh=kh),
        grid=(n_heads, n_rows // rows_per_step),
        in_specs=[
            pl.BlockSpec((tq, HEAD_DIM), lambda hd, rb: (rb, hd)),
            pl.BlockSpec((s, HEAD_DIM), lambda hd, rb: (0, n_heads + hd)),
            pl.BlockSpec((s, HEAD_DIM), lambda hd, rb: (0, 2 * n_heads + hd)),
            pl.BlockSpec((None, None, kh, GRID_W, kh * GRID_W), lambda hd, rb: (layer, hd, 0, 0, 0)),
        ],
        out_specs=pl.BlockSpec((tq, HEAD_DIM), lambda hd, rb: (rb, hd)),
        out_shape=jax.ShapeDtypeStruct((s, n_heads * HEAD_DIM), BF16),
        compiler_params=pltpu.CompilerParams(
            dimension_semantics=("parallel", "arbitrary"), vmem_limit_bytes=VMEM_LIMIT),
        name="na_attn",
    )(h, h, h, bias)


def _na_bias_table(rel_bias, n_rows):
    kh = min(NA_KH, n_rows)
    w, kw = GRID_W, NA_KW
    nl, nh, nd, _ = rel_bias.shape
    rb = rel_bias.astype(F32)
    ring = jnp.concatenate([
        rb[..., kw - 1:],
        jnp.repeat(rb[..., -1:], w - kw + 1, axis=-1),
        jnp.repeat(rb[..., :1], w - kw + 1, axis=-1),
        rb[..., 1:kw - 1],
    ], axis=-1)
    toep = jnp.tile(ring, (1, 1, 1, w))[..., :w * (2 * w - 1)]
    toep = toep.reshape(nl, nh, nd, w, 2 * w - 1)[..., :w]
    c = jnp.arange(w)
    c0 = jnp.clip(c - kw // 2, 0, w - kw)
    col_in = (c[None, :] >= c0[:, None]) & (c[None, :] < c0[:, None] + kw)
    toep = jnp.where(col_in, toep * LOG2E, MASK_VALUE)
    tab = jnp.stack([toep[:, :, NA_KH - 1 - p:NA_KH - 1 - p + kh] for p in range(kh)], axis=2)
    tab = jnp.transpose(tab, (0, 1, 2, 4, 3, 5))
    return tab.reshape(nl, nh, kh, w, kh * w)


def _gqa_kernel(qt_ref, k_ref, vt_ref, o_ref, sa_ref, sb_ref, *, tk, nk):
    tq = qt_ref.shape[1]
    qt = qt_ref[...]

    def scores(c, dst_ref):
        start = pl.multiple_of(c * tk, tk)
        st = jnp.dot(k_ref[pl.ds(start, tk), :], qt, preferred_element_type=F32)
        dst_ref[...] = st
        return jnp.max(st.reshape(tk // 8, 8, tq), axis=0)

    def softmax_step(c, src_ref, m8, m, l8, acc):
        m_new = jnp.maximum(m, jnp.max(m8, axis=0, keepdims=True))
        corr = jnp.exp2(m - m_new)
        p = jnp.exp2(src_ref[...] - m_new)
        l8 = corr * l8 + jnp.sum(p.reshape(tk // 8, 8, tq), axis=0)
        acc = corr * acc + jnp.dot(vt_ref[c], p.astype(BF16), preferred_element_type=F32)
        return m_new, l8, acc

    def pair(c, m8a, m, l8, acc, last):
        m8b = scores(c + 1, sb_ref)
        m, l8, acc = softmax_step(c, sa_ref, m8a, m, l8, acc)
        if not last:
            m8a = scores(c + 2, sa_ref)
        m, l8, acc = softmax_step(c + 1, sb_ref, m8b, m, l8, acc)
        return m8a, m, l8, acc

    init = (scores(0, sa_ref), jnp.full((1, tq), -jnp.inf, F32), jnp.zeros((8, tq), F32),
            jnp.zeros((HEAD_DIM, tq), F32))
    carry = lax.fori_loop(0, nk // 2 - 1, lambda j, cr: pair(2 * j, *cr, last=False), init)
    _, _, l8, acc = pair(nk - 2, *carry, last=True)
    ot = acc / jnp.sum(l8, axis=0, keepdims=True)
    o_ref[...] = ot.T.astype(o_ref.dtype)


def _gqa_attn(h, d_na):
    s = h.shape[0]
    d_gqa = d_na
    n_heads = d_gqa // HEAD_DIM
    group = n_heads // N_KV_HEADS
    tq, tk = GQA_TQ, GQA_TK
    assert s % tq == 0 and s % (2 * tk) == 0
    nk = s // tk
    q0 = 3 * d_na
    k0 = q0 + d_gqa
    v0 = k0 + N_KV_HEADS * HEAD_DIM
    qt = h[:, q0:k0].T
    vt = h[:, v0:].reshape(nk, tk, N_KV_HEADS, HEAD_DIM).transpose(2, 0, 3, 1)
    return pl.pallas_call(
        functools.partial(_gqa_kernel, tk=tk, nk=nk),
        grid=(n_heads, s // tq),
        in_specs=[
            pl.BlockSpec((HEAD_DIM, tq), lambda hd, i: (hd, i)),
            pl.BlockSpec((s, HEAD_DIM), lambda hd, i: (0, k0 // HEAD_DIM + hd // group)),
            pl.BlockSpec((None, nk, HEAD_DIM, tk), lambda hd, i: (hd // group, 0, 0, 0)),
        ],
        out_specs=pl.BlockSpec((tq, HEAD_DIM), lambda hd, i: (i, hd)),
        out_shape=jax.ShapeDtypeStruct((s, d_gqa), BF16),
        scratch_shapes=[pltpu.VMEM((tk, tq), F32), pltpu.VMEM((tk, tq), F32)],
        compiler_params=pltpu.CompilerParams(
            dimension_semantics=("parallel", "arbitrary"), vmem_limit_bytes=VMEM_LIMIT),
        name="gqa_attn",
    )(qt, h, vt)


def _out_proj_ln_kernel(ona_ref, og_ref, x_ref, w_ref, gna_ref, gg_ref, g_ref, b_ref, o_ref, *, alpha):
    d_na = ona_ref.shape[1]
    a = ona_ref[...].astype(F32)
    a = (a * _rms_scale(a) * gna_ref[...]).astype(BF16)
    b = og_ref[...].astype(F32)
    b = (b * _rms_scale(b) * gg_ref[...]).astype(BF16)
    mix = jnp.dot(a, w_ref[:d_na, :], preferred_element_type=F32)
    mix = mix + jnp.dot(b, w_ref[d_na:, :], preferred_element_type=F32)
    y = alpha * x_ref[...] + mix
    o_ref[...] = _layer_norm(y, g_ref[...], b_ref[...])


def _out_proj_ln(o_na, o_g, x, w_out, gn_na, gn_g, ln_g, ln_b, layer, alpha):
    s, d = x.shape
    d_na = o_na.shape[1]
    d_g = o_g.shape[1]
    tm = PROJ_TM
    return pl.pallas_call(
        functools.partial(_out_proj_ln_kernel, alpha=alpha),
        grid=(s // tm,),
        in_specs=[
            pl.BlockSpec((tm, d_na), lambda i: (i, 0)),
            pl.BlockSpec((tm, d_g), lambda i: (i, 0)),
            pl.BlockSpec((tm, d), lambda i: (i, 0)),
            pl.BlockSpec((None, d_na + d_g, d), lambda i: (layer, 0, 0)),
            pl.BlockSpec((None, 1, d_na), lambda i: (layer, 0, 0)),
            pl.BlockSpec((None, 1, d_g), lambda i: (layer, 0, 0)),
            pl.BlockSpec((None, 1, d), lambda i: (layer, 0, 0)),
            pl.BlockSpec((None, 1, d), lambda i: (layer, 0, 0)),
        ],
        out_specs=pl.BlockSpec((tm, d), lambda i: (i, 0)),
        out_shape=jax.ShapeDtypeStruct((s, d), F32),
        compiler_params=pltpu.CompilerParams(
            dimension_semantics=("parallel",), vmem_limit_bytes=VMEM_LIMIT),
        name="out_proj_ln",
    )(o_na, o_g, x, w_out, gn_na, gn_g, ln_g, ln_b)


def _rope_tables(s):
    half = HEAD_DIM // 2
    nfreq = half // 2
    t = jnp.arange(s)
    row = (t // GRID_W).astype(F32)
    col = (t % GRID_W).astype(F32)
    inv_freq = 1.0 / (ROPE_THETA ** (jnp.arange(nfreq, dtype=F32) / nfreq))
    lane = jnp.arange(HEAD_DIM)
    pos = jnp.where(lane[None, :] < half, row[:, None], col[:, None])
    ang = pos * inv_freq[lane % nfreq][None, :]
    cos, sin = jnp.cos(ang), jnp.sin(ang)
    lower = (lane % half) < nfreq
    sin_lo = jnp.where(lower[None, :], -sin, 0.0)
    sin_hi = jnp.where(lower[None, :], 0.0, sin)
    return cos, sin_lo, sin_hi


def kernel(x, ffn1_w_gate_up, ffn1_w_down, ln1_g, ln1_b, w_in, na_rel_bias, q_norm_g, k_norm_g,
           gn_na_g, gn_gqa_g, w_out, ln2_g, ln2_b, ffn2_w_gate_up, ffn2_w_down, ln3_g, ln3_b):
    batch, s, d = x.shape
    depth = w_in.shape[0]
    alpha = (2.0 * depth) ** 0.25
    d_na = d // 2
    n_heads_na = d_na // HEAD_DIM

    def row(p):
        return p.astype(F32)[:, None, :]

    wgu1, wd1 = ffn1_w_gate_up.astype(BF16), ffn1_w_down.astype(BF16)
    wgu2, wd2 = ffn2_w_gate_up.astype(BF16), ffn2_w_down.astype(BF16)
    w_in_b, w_out_b = w_in.astype(BF16), w_out.astype(BF16)
    cos, sin_lo, sin_hi = _rope_tables(s)
    bias = _na_bias_table(na_rel_bias, s // GRID_W)
    ln1g, ln1b, ln2g, ln2b, ln3g, ln3b = map(row, (ln1_g, ln1_b, ln2_g, ln2_b, ln3_g, ln3_b))
    qg, kg, gna, ggq = map(row, (q_norm_g, k_norm_g, gn_na_g, gn_gqa_g))

    outs = []
    for bi in range(batch):
        xs = x[bi]
        for layer in range(depth):
            xs = _ffn_ln(xs, wgu1, wd1, ln1g, ln1b, layer, alpha)
            h = _in_proj(xs, w_in_b, cos, sin_lo, sin_hi, qg, kg, layer)
            o_na = _na_attn(h, bias, layer, n_heads_na)
            o_g = _gqa_attn(h, d_na)
            xs = _out_proj_ln(o_na, o_g, xs, w_out_b, gna, ggq, ln2g, ln2b, layer, alpha)
            xs = _ffn_ln(xs, wgu2, wd2, ln3g, ln3b, layer, alpha)
        outs.append(xs)
    return outs[0][None] if batch == 1 else jnp.stack(outs, axis=0)
```

```python
import functools

import jax
import jax.numpy as jnp
from jax import lax
from jax.experimental import pallas as pl
from jax.experimental.pallas import tpu as pltpu

F32 = jnp.float32
BF16 = jnp.bfloat16

GRID_W = 64
HEAD_DIM = 128
N_KV_HEADS = 2
NA_KH = 8
NA_KW = 16
ROPE_THETA = 10000.0
RMS_EPS = 1e-6
LN_EPS = 1e-5
LOG2E = 1.4426950408889634
SM_SCALE = HEAD_DIM ** -0.5 * LOG2E
MASK_VALUE = -1e30

FFN_TM = 512
FFN_TF = 512
PROJ_TM = 512
PROJ_TN = 256
PROJ_SUB = 256
GQA_TQ = 1024
GQA_TK = 512
NA_G = 4
NA_SLAB = 12
NA_BLOCKS = 8
VMEM_LIMIT = 56 * 1024 * 1024


def _layer_norm(y, g, b):
    mu = jnp.mean(y, axis=-1, keepdims=True)
    yc = y - mu
    var = jnp.mean(yc * yc, axis=-1, keepdims=True)
    return yc * lax.rsqrt(var + LN_EPS) * g + b


def _rms_scale(t):
    return lax.rsqrt(jnp.mean(t * t, axis=-1, keepdims=True) + RMS_EPS)


def _ffn_ln_kernel(x_ref, wg_ref, wu_ref, wd_ref, g_ref, b_ref, o_ref, xb_ref, acc_ref, *, alpha, nf):
    j = pl.program_id(1)

    @pl.when(j == 0)
    def _():
        xb_ref[...] = x_ref[...].astype(BF16)
        acc_ref[...] = jnp.zeros_like(acc_ref)

    xb = xb_ref[...]
    gate = jnp.dot(xb, wg_ref[...], preferred_element_type=F32)
    up = jnp.dot(xb, wu_ref[...], preferred_element_type=F32)
    hidden = (gate / (1.0 + jnp.exp(-gate)) * up).astype(BF16)
    acc_ref[...] += jnp.dot(hidden, wd_ref[...], preferred_element_type=F32)

    @pl.when(j == nf - 1)
    def _():
        y = alpha * x_ref[...] + 0.5 * acc_ref[...]
        o_ref[...] = _layer_norm(y, g_ref[...], b_ref[...])


def _ffn_ln(x, w_gu, w_d, ln_g, ln_b, layer, alpha):
    s, d = x.shape
    d_ff = w_d.shape[1]
    tm, tf = FFN_TM, FFN_TF
    assert s % tm == 0 and d_ff % tf == 0
    nf = d_ff // tf
    return pl.pallas_call(
        functools.partial(_ffn_ln_kernel, alpha=alpha, nf=nf),
        grid=(s // tm, nf),
        in_specs=[
            pl.BlockSpec((tm, d), lambda i, j: (i, 0)),
            pl.BlockSpec((None, d, tf), lambda i, j: (layer, 0, j)),
            pl.BlockSpec((None, d, tf), lambda i, j: (layer, 0, j + nf)),
            pl.BlockSpec((None, tf, d), lambda i, j: (layer, j, 0)),
            pl.BlockSpec((None, 1, d), lambda i, j: (layer, 0, 0)),
            pl.BlockSpec((None, 1, d), lambda i, j: (layer, 0, 0)),
        ],
        out_specs=pl.BlockSpec((tm, d), lambda i, j: (i, 0)),
        out_shape=jax.ShapeDtypeStruct((s, d), F32),
        scratch_shapes=[pltpu.VMEM((tm, d), BF16), pltpu.VMEM((tm, d), F32)],
        compiler_params=pltpu.CompilerParams(
            dimension_semantics=("parallel", "arbitrary"), vmem_limit_bytes=VMEM_LIMIT),
        name="ffn_ln",
    )(x, w_gu, w_gu, w_d, ln_g, ln_b)


def _rope(t, cos, sin_lo, sin_hi):
    return t * cos + pltpu.roll(t, 96, 1) * sin_lo + pltpu.roll(t, 32, 1) * sin_hi


def _in_proj_kernel(x_ref, w_ref, cos_ref, slo_ref, shi_ref, qg_ref, kg_ref,
                    hna_ref, qt_ref, kgo_ref, vt_ref, xb_ref, *, d_na):
    xb_ref[...] = x_ref[...].astype(BF16)
    tn = PROJ_TN
    heads = tn // HEAD_DIM

    def cols(c0):
        return jnp.dot(xb_ref[...], w_ref[:, c0:c0 + tn], preferred_element_type=F32)

    def head(acc, h):
        return acc[:, h * HEAD_DIM:(h + 1) * HEAD_DIM]

    def normed_rope(t, gain):
        t = t * _rms_scale(t) * gain
        return _rope(t, cos_ref[...], slo_ref[...], shi_ref[...])

    for c in range(0, d_na, tn):
        acc = cols(3 * d_na + c)
        for h in range(heads):
            t = normed_rope(head(acc, h), qg_ref[...]) * SM_SCALE
            qt_ref[c + h * HEAD_DIM:c + (h + 1) * HEAD_DIM, :] = t.T.astype(qt_ref.dtype)
    acc = cols(4 * d_na)
    for h in range(N_KV_HEADS):
        kgo_ref[:, h * HEAD_DIM:(h + 1) * HEAD_DIM] = normed_rope(head(acc, h), kg_ref[...]).astype(kgo_ref.dtype)
    acc = cols(4 * d_na + N_KV_HEADS * HEAD_DIM)
    for h in range(N_KV_HEADS):
        vt_ref[h, 0] = head(acc, h).T.astype(vt_ref.dtype)
    for c in range(0, d_na, tn):
        hna_ref[:, c:c + tn] = (cols(c) * SM_SCALE).astype(hna_ref.dtype)
    for c in range(d_na, 3 * d_na, tn):
        hna_ref[:, c:c + tn] = cols(c).astype(hna_ref.dtype)


def _in_proj(x, w_in, cos, sin_lo, sin_hi, q_gain, k_gain, layer):
    s, d = x.shape
    d_in = w_in.shape[2]
    tm, tn = PROJ_TM, PROJ_TN
    d_na = d // 2
    d_kv = N_KV_HEADS * HEAD_DIM
    assert d_na % tn == 0 and d_kv == tn and d_in == 4 * d_na + 2 * d_kv and tm == GQA_TK
    return pl.pallas_call(
        functools.partial(_in_proj_kernel, d_na=d_na),
        grid=(s // tm,),
        in_specs=[
            pl.BlockSpec((tm, d), lambda i: (i, 0)),
            pl.BlockSpec((None, d, d_in), lambda i: (layer, 0, 0), pipeline_mode=pl.Buffered(1)),
            pl.BlockSpec((tm, HEAD_DIM), lambda i: (i, 0)),
            pl.BlockSpec((tm, HEAD_DIM), lambda i: (i, 0)),
            pl.BlockSpec((tm, HEAD_DIM), lambda i: (i, 0)),
            pl.BlockSpec((None, 1, HEAD_DIM), lambda i: (layer, 0, 0)),
            pl.BlockSpec((None, 1, HEAD_DIM), lambda i: (layer, 0, 0)),
        ],
        out_specs=[
            pl.BlockSpec((tm, 3 * d_na), lambda i: (i, 0)),
            pl.BlockSpec((d_na, tm), lambda i: (0, i)),
            pl.BlockSpec((tm, d_kv), lambda i: (i, 0)),
            pl.BlockSpec((N_KV_HEADS, 1, HEAD_DIM, tm), lambda i: (0, i, 0, 0)),
        ],
        out_shape=[
            jax.ShapeDtypeStruct((s, 3 * d_na), BF16),
            jax.ShapeDtypeStruct((d_na, s), BF16),
            jax.ShapeDtypeStruct((s, d_kv), BF16),
            jax.ShapeDtypeStruct((N_KV_HEADS, s // tm, HEAD_DIM, tm), BF16),
        ],
        scratch_shapes=[pltpu.VMEM((tm, d), BF16)],
        compiler_params=pltpu.CompilerParams(
            dimension_semantics=("parallel",), vmem_limit_bytes=VMEM_LIMIT),
        name="in_proj",
    )(x, w_in, cos, sin_lo, sin_hi, q_gain, k_gain)


def _na_kernel(q_ref, k_ref, v_ref, bias_ref, o_ref, *, blocks_per_step, n_rows):
    step = pl.program_id(1)
    n_blocks = n_rows // NA_G
    tq = NA_G * GRID_W
    slab = NA_SLAB * GRID_W

    def scores(i):
        b = step * blocks_per_step + i
        start_row = jnp.clip(b * NA_G - NA_KH // 2, 0, n_rows - NA_SLAB)
        start = pl.multiple_of(start_row * GRID_W, GRID_W)
        q = q_ref[i * tq:(i + 1) * tq, :]
        k = k_ref[pl.ds(start, slab), :]
        s = lax.dot_general(q, k, (((1,), (1,)), ((), ())), preferred_element_type=F32)
        pattern = jnp.where(b == 0, 0, jnp.where(b == n_blocks - 1, 2, 1))
        return s + bias_ref[pattern], start

    nxt = scores(0)
    for i in range(blocks_per_step):
        s, start = nxt
        if i + 1 < blocks_per_step:
            nxt = scores(i + 1)
        m = jnp.max(s, axis=-1, keepdims=True)
        p = jnp.exp2(s - m)
        l = jnp.sum(p, axis=-1, keepdims=True)
        v = v_ref[pl.ds(start, slab), :]
        o = jnp.dot(p.astype(BF16), v, preferred_element_type=F32) / l
        o_ref[i * tq:(i + 1) * tq, :] = o.astype(o_ref.dtype)


def _na_attn(h, bias, layer, n_heads):
    s = h.shape[0]
    n_rows = s // GRID_W
    blocks_per_step = NA_BLOCKS
    rows_per_step = blocks_per_step * NA_G
    assert n_rows % rows_per_step == 0 and n_rows >= NA_SLAB
    tq = rows_per_step * GRID_W
    return pl.pallas_call(
        functools.partial(_na_kernel, blocks_per_step=blocks_per_step, n_rows=n_rows),
        grid=(n_heads, n_rows // rows_per_step),
        in_specs=[
            pl.BlockSpec((tq, HEAD_DIM), lambda hd, rb: (rb, hd)),
            pl.BlockSpec((s, HEAD_DIM), lambda hd, rb: (0, n_heads + hd)),
            pl.BlockSpec((s, HEAD_DIM), lambda hd, rb: (0, 2 * n_heads + hd)),
            pl.BlockSpec((None, None, 3, NA_G * GRID_W, NA_SLAB * GRID_W),
                         lambda hd, rb: (layer, hd, 0, 0, 0)),
        ],
        out_specs=pl.BlockSpec((tq, HEAD_DIM), lambda hd, rb: (rb, hd)),
        out_shape=jax.ShapeDtypeStruct((s, n_heads * HEAD_DIM), BF16),
        compiler_params=pltpu.CompilerParams(
            dimension_semantics=("parallel", "arbitrary"), vmem_limit_bytes=VMEM_LIMIT),
        name="na_attn",
    )(h, h, h, bias)


def _na_bias_table(rel_bias, n_rows):
    w, kw = GRID_W, NA_KW
    nl, nh, nd, _ = rel_bias.shape
    rb = rel_bias.astype(F32)
    ring = jnp.concatenate([
        rb[..., kw - 1:],
        jnp.repeat(rb[..., -1:], w - kw + 1, axis=-1),
        jnp.repeat(rb[..., :1], w - kw + 1, axis=-1),
        rb[..., 1:kw - 1],
    ], axis=-1)
    toep = jnp.tile(ring, (1, 1, 1, w))[..., :w * (2 * w - 1)]
    toep = toep.reshape(nl, nh, nd, w, 2 * w - 1)[..., :w]
    c = jnp.arange(w)
    c0 = jnp.clip(c - kw // 2, 0, w - kw)
    col_in = (c[None, :] >= c0[:, None]) & (c[None, :] < c0[:, None] + kw)
    toep = jnp.where(col_in, toep * LOG2E, MASK_VALUE)
    masked = jnp.full((nl, nh, w, w), MASK_VALUE, F32)
    first, interior, last = (0, 0), (1, -(NA_KH // 2)), (n_rows // NA_G - 1, NA_G - NA_SLAB)
    pats = []
    for b, rel in (first, interior, last):
        rows = []
        for rr in range(NA_G):
            r = b * NA_G + rr
            r0 = min(max(r - NA_KH // 2, 0), n_rows - NA_KH)
            lo = r0 - (b * NA_G + rel)
            rows.append(jnp.stack(
                [toep[:, :, u + rel - rr + NA_KH - 1] if lo <= u < lo + NA_KH else masked
                 for u in range(NA_SLAB)], axis=3))
        pats.append(jnp.stack(rows, axis=2))
    tab = jnp.stack(pats, axis=2)
    return tab.reshape(nl, nh, 3, NA_G * w, NA_SLAB * w)


def _gqa_kernel(qt_ref, k_ref, vt_ref, o_ref, sa_ref, sb_ref, *, tk, nk):
    tq = qt_ref.shape[1]
    qt = qt_ref[...]

    def scores(c, dst_ref):
        start = pl.multiple_of(c * tk, tk)
        st = jnp.dot(k_ref[pl.ds(start, tk), :], qt, preferred_element_type=F32)
        dst_ref[...] = st
        return jnp.max(st.reshape(tk // 8, 8, tq), axis=0)

    def softmax_step(c, src_ref, m8, m, l8, acc):
        m_new = jnp.maximum(m, jnp.max(m8, axis=0, keepdims=True))
        corr = jnp.exp2(m - m_new)
        p = jnp.exp2(src_ref[...] - m_new)
        l8 = corr * l8 + jnp.sum(p.reshape(tk // 8, 8, tq), axis=0)
        acc = corr * acc + jnp.dot(vt_ref[c], p.astype(BF16), preferred_element_type=F32)
        return m_new, l8, acc

    def pair(c, m8a, m, l8, acc, last):
        m8b = scores(c + 1, sb_ref)
        m, l8, acc = softmax_step(c, sa_ref, m8a, m, l8, acc)
        if not last:
            m8a = scores(c + 2, sa_ref)
        m, l8, acc = softmax_step(c + 1, sb_ref, m8b, m, l8, acc)
        return m8a, m, l8, acc

    init = (scores(0, sa_ref), jnp.full((1, tq), -jnp.inf, F32), jnp.zeros((8, tq), F32),
            jnp.zeros((HEAD_DIM, tq), F32))
    carry = lax.fori_loop(0, nk // 2 - 1, lambda j, cr: pair(2 * j, *cr, last=False), init)
    _, _, l8, acc = pair(nk - 2, *carry, last=True)
    ot = acc / jnp.sum(l8, axis=0, keepdims=True)
    o_ref[...] = ot.T.astype(o_ref.dtype)


def _gqa_attn(qt, kg, vt):
    d_gqa, s = qt.shape
    n_heads = d_gqa // HEAD_DIM
    group = n_heads // N_KV_HEADS
    tq, tk = GQA_TQ, GQA_TK
    assert s % tq == 0 and s % (2 * tk) == 0
    nk = s // tk
    return pl.pallas_call(
        functools.partial(_gqa_kernel, tk=tk, nk=nk),
        grid=(n_heads, s // tq),
        in_specs=[
            pl.BlockSpec((HEAD_DIM, tq), lambda hd, i: (hd, i)),
            pl.BlockSpec((s, HEAD_DIM), lambda hd, i: (0, hd // group)),
            pl.BlockSpec((None, nk, HEAD_DIM, tk), lambda hd, i: (hd // group, 0, 0, 0)),
        ],
        out_specs=pl.BlockSpec((tq, HEAD_DIM), lambda hd, i: (i, hd)),
        out_shape=jax.ShapeDtypeStruct((s, d_gqa), BF16),
        scratch_shapes=[pltpu.VMEM((tk, tq), F32), pltpu.VMEM((tk, tq), F32)],
        compiler_params=pltpu.CompilerParams(
            dimension_semantics=("parallel", "arbitrary"), vmem_limit_bytes=VMEM_LIMIT),
        name="gqa_attn",
    )(qt, kg, vt)


def _out_proj_ln_kernel(ona_ref, og_ref, x_ref, w_ref, gna_ref, gg_ref, g_ref, b_ref, o_ref, *, alpha):
    tm, d_na = ona_ref.shape
    subs = [slice(r0, r0 + PROJ_SUB) for r0 in range(0, tm, PROJ_SUB)]
    normed = []
    for rows in subs:
        a = ona_ref[rows, :].astype(F32)
        b = og_ref[rows, :].astype(F32)
        normed.append(((a * _rms_scale(a) * gna_ref[...]).astype(BF16),
                       (b * _rms_scale(b) * gg_ref[...]).astype(BF16)))
    mixes = [jnp.dot(a, w_ref[:d_na, :], preferred_element_type=F32)
             + jnp.dot(b, w_ref[d_na:, :], preferred_element_type=F32) for a, b in normed]
    for rows, mix in zip(subs, mixes):
        y = alpha * x_ref[rows, :] + mix
        o_ref[rows, :] = _layer_norm(y, g_ref[...], b_ref[...])


def _out_proj_ln(o_na, o_g, x, w_out, gn_na, gn_g, ln_g, ln_b, layer, alpha):
    s, d = x.shape
    d_na = o_na.shape[1]
    d_g = o_g.shape[1]
    tm = PROJ_TM
    return pl.pallas_call(
        functools.partial(_out_proj_ln_kernel, alpha=alpha),
        grid=(s // tm,),
        in_specs=[
            pl.BlockSpec((tm, d_na), lambda i: (i, 0)),
            pl.BlockSpec((tm, d_g), lambda i: (i, 0)),
            pl.BlockSpec((tm, d), lambda i: (i, 0)),
            pl.BlockSpec((None, d_na + d_g, d), lambda i: (layer, 0, 0)),
            pl.BlockSpec((None, 1, d_na), lambda i: (layer, 0, 0)),
            pl.BlockSpec((None, 1, d_g), lambda i: (layer, 0, 0)),
            pl.BlockSpec((None, 1, d), lambda i: (layer, 0, 0)),
            pl.BlockSpec((None, 1, d), lambda i: (layer, 0, 0)),
        ],
        out_specs=pl.BlockSpec((tm, d), lambda i: (i, 0)),
        out_shape=jax.ShapeDtypeStruct((s, d), F32),
        compiler_params=pltpu.CompilerParams(
            dimension_semantics=("parallel",), vmem_limit_bytes=VMEM_LIMIT),
        name="out_proj_ln",
    )(o_na, o_g, x, w_out, gn_na, gn_g, ln_g, ln_b)


def _rope_tables(s):
    half = HEAD_DIM // 2
    nfreq = half // 2
    t = jnp.arange(s)
    row = (t // GRID_W).astype(F32)
    col = (t % GRID_W).astype(F32)
    inv_freq = 1.0 / (ROPE_THETA ** (jnp.arange(nfreq, dtype=F32) / nfreq))
    lane = jnp.arange(HEAD_DIM)
    pos = jnp.where(lane[None, :] < half, row[:, None], col[:, None])
    ang = pos * inv_freq[lane % nfreq][None, :]
    cos, sin = jnp.cos(ang), jnp.sin(ang)
    lower = (lane % half) < nfreq
    sin_lo = jnp.where(lower[None, :], -sin, 0.0)
    sin_hi = jnp.where(lower[None, :], 0.0, sin)
    return cos, sin_lo, sin_hi


def kernel(x, ffn1_w_gate_up, ffn1_w_down, ln1_g, ln1_b, w_in, na_rel_bias, q_norm_g, k_norm_g,
           gn_na_g, gn_gqa_g, w_out, ln2_g, ln2_b, ffn2_w_gate_up, ffn2_w_down, ln3_g, ln3_b):
    batch, s, d = x.shape
    depth = w_in.shape[0]
    alpha = (2.0 * depth) ** 0.25
    d_na = d // 2
    n_heads_na = d_na // HEAD_DIM

    def row(p):
        return p.astype(F32)[:, None, :]

    wgu1, wd1 = ffn1_w_gate_up.astype(BF16), ffn1_w_down.astype(BF16)
    wgu2, wd2 = ffn2_w_gate_up.astype(BF16), ffn2_w_down.astype(BF16)
    w_in_b, w_out_b = w_in.astype(BF16), w_out.astype(BF16)
    cos, sin_lo, sin_hi = _rope_tables(s)
    bias = _na_bias_table(na_rel_bias, s // GRID_W)
    ln1g, ln1b, ln2g, ln2b, ln3g, ln3b = map(row, (ln1_g, ln1_b, ln2_g, ln2_b, ln3_g, ln3_b))
    qg, kg, gna, ggq = map(row, (q_norm_g, k_norm_g, gn_na_g, gn_gqa_g))

    outs = []
    for bi in range(batch):
        xs = x[bi]
        for layer in range(depth):
            xs = _ffn_ln(xs, wgu1, wd1, ln1g, ln1b, layer, alpha)
            h_na, q_t, k_g, v_t = _in_proj(xs, w_in_b, cos, sin_lo, sin_hi, qg, kg, layer)
            o_na = _na_attn(h_na, bias, layer, n_heads_na)
            o_g = _gqa_attn(q_t, k_g, v_t)
            xs = _out_proj_ln(o_na, o_g, xs, w_out_b, gna, ggq, ln2g, ln2b, layer, alpha)
            xs = _ffn_ln(xs, wgu2, wd2, ln3g, ln3b, layer, alpha)
        outs.append(xs)
    return outs[0][None] if batch == 1 else jnp.stack(outs, axis=0)
```

```python
import functools

import jax
import jax.numpy as jnp
from jax import lax
from jax.experimental import pallas as pl
from jax.experimental.pallas import tpu as pltpu

F32 = jnp.float32
BF16 = jnp.bfloat16

GRID_W = 64
HEAD_DIM = 128
N_KV_HEADS = 2
NA_KH = 8
NA_KW = 16
ROPE_THETA = 10000.0
RMS_EPS = 1e-6
LN_EPS = 1e-5
LOG2E = 1.4426950408889634
SM_SCALE = HEAD_DIM ** -0.5 * LOG2E
MASK_VALUE = -1e30

FFN_TM = 512
FFN_TF = 512
PROJ_TM = 512
PROJ_TN = 256
PROJ_SUB = 256
GQA_TQ = 2048
GQA_TK = 512
GQA_QB = 256
NA_G = 4
NA_SLAB = 12
NA_BLOCKS = 8
VMEM_LIMIT = 56 * 1024 * 1024


def _layer_norm(y, g, b):
    mu = jnp.mean(y, axis=-1, keepdims=True)
    yc = y - mu
    var = jnp.mean(yc * yc, axis=-1, keepdims=True)
    return yc * lax.rsqrt(var + LN_EPS) * g + b


def _rms_scale(t):
    return lax.rsqrt(jnp.mean(t * t, axis=-1, keepdims=True) + RMS_EPS)


def _ffn_ln_kernel(x_ref, wg_ref, wu_ref, wd_ref, g_ref, b_ref, o_ref, xb_ref, acc_ref, *, alpha, nf):
    j = pl.program_id(1)

    @pl.when(j == 0)
    def _():
        xb_ref[...] = x_ref[...].astype(BF16)
        acc_ref[...] = jnp.zeros_like(acc_ref)

    xb = xb_ref[...]
    gate = jnp.dot(xb, wg_ref[...], preferred_element_type=F32)
    up = jnp.dot(xb, wu_ref[...], preferred_element_type=F32)
    hidden = (gate / (1.0 + jnp.exp(-gate)) * up).astype(BF16)
    acc_ref[...] += jnp.dot(hidden, wd_ref[...], preferred_element_type=F32)

    @pl.when(j == nf - 1)
    def _():
        y = alpha * x_ref[...] + 0.5 * acc_ref[...]
        o_ref[...] = _layer_norm(y, g_ref[...], b_ref[...])


def _ffn_ln(x, w_gu, w_d, ln_g, ln_b, layer, alpha):
    s, d = x.shape
    d_ff = w_d.shape[1]
    tm, tf = FFN_TM, FFN_TF
    assert s % tm == 0 and d_ff % tf == 0
    nf = d_ff // tf
    return pl.pallas_call(
        functools.partial(_ffn_ln_kernel, alpha=alpha, nf=nf),
        grid=(s // tm, nf),
        in_specs=[
            pl.BlockSpec((tm, d), lambda i, j: (i, 0)),
            pl.BlockSpec((None, d, tf), lambda i, j: (layer, 0, j)),
            pl.BlockSpec((None, d, tf), lambda i, j: (layer, 0, j + nf)),
            pl.BlockSpec((None, tf, d), lambda i, j: (layer, j, 0)),
            pl.BlockSpec((None, 1, d), lambda i, j: (layer, 0, 0)),
            pl.BlockSpec((None, 1, d), lambda i, j: (layer, 0, 0)),
        ],
        out_specs=pl.BlockSpec((tm, d), lambda i, j: (i, 0)),
        out_shape=jax.ShapeDtypeStruct((s, d), F32),
        scratch_shapes=[pltpu.VMEM((tm, d), BF16), pltpu.VMEM((tm, d), F32)],
        compiler_params=pltpu.CompilerParams(
            dimension_semantics=("parallel", "arbitrary"), vmem_limit_bytes=VMEM_LIMIT),
        name="ffn_ln",
    )(x, w_gu, w_gu, w_d, ln_g, ln_b)


def _rope(t, cos, sin_lo, sin_hi):
    return t * cos + pltpu.roll(t, 96, 1) * sin_lo + pltpu.roll(t, 32, 1) * sin_hi


def _in_proj_kernel(x_ref, w_ref, cos_ref, slo_ref, shi_ref, qg_ref, kg_ref,
                    hna_ref, qt_ref, kgo_ref, vt_ref, xb_ref, *, d_na):
    xb_ref[...] = x_ref[...].astype(BF16)
    tn = PROJ_TN
    heads = tn // HEAD_DIM

    def cols(c0):
        return jnp.dot(xb_ref[...], w_ref[:, c0:c0 + tn], preferred_element_type=F32)

    def head(acc, h):
        return acc[:, h * HEAD_DIM:(h + 1) * HEAD_DIM]

    def normed_rope(t, gain):
        t = t * _rms_scale(t) * gain
        return _rope(t, cos_ref[...], slo_ref[...], shi_ref[...])

    for c in range(0, d_na, tn):
        acc = cols(3 * d_na + c)
        for h in range(heads):
            t = normed_rope(head(acc, h), qg_ref[...]) * SM_SCALE
            qt_ref[c + h * HEAD_DIM:c + (h + 1) * HEAD_DIM, :] = t.T.astype(qt_ref.dtype)
    acc = cols(4 * d_na)
    for h in range(N_KV_HEADS):
        kgo_ref[:, h * HEAD_DIM:(h + 1) * HEAD_DIM] = normed_rope(head(acc, h), kg_ref[...]).astype(kgo_ref.dtype)
    acc = cols(4 * d_na + N_KV_HEADS * HEAD_DIM)
    for h in range(N_KV_HEADS):
        vt_ref[h, 0] = head(acc, h).T.astype(vt_ref.dtype)
    for c in range(0, d_na, tn):
        hna_ref[:, c:c + tn] = (cols(c) * SM_SCALE).astype(hna_ref.dtype)
    for c in range(d_na, 3 * d_na, tn):
        hna_ref[:, c:c + tn] = cols(c).astype(hna_ref.dtype)


def _in_proj(x, w_in, cos, sin_lo, sin_hi, q_gain, k_gain, layer):
    s, d = x.shape
    d_in = w_in.shape[2]
    tm, tn = PROJ_TM, PROJ_TN
    d_na = d // 2
    d_kv = N_KV_HEADS * HEAD_DIM
    assert d_na % tn == 0 and d_kv == tn and d_in == 4 * d_na + 2 * d_kv and tm == GQA_TK
    return pl.pallas_call(
        functools.partial(_in_proj_kernel, d_na=d_na),
        grid=(s // tm,),
        in_specs=[
            pl.BlockSpec((tm, d), lambda i: (i, 0)),
            pl.BlockSpec((None, d, d_in), lambda i: (layer, 0, 0), pipeline_mode=pl.Buffered(1)),
            pl.BlockSpec((tm, HEAD_DIM), lambda i: (i, 0)),
            pl.BlockSpec((tm, HEAD_DIM), lambda i: (i, 0)),
            pl.BlockSpec((tm, HEAD_DIM), lambda i: (i, 0)),
            pl.BlockSpec((None, 1, HEAD_DIM), lambda i: (layer, 0, 0)),
            pl.BlockSpec((None, 1, HEAD_DIM), lambda i: (layer, 0, 0)),
        ],
        out_specs=[
            pl.BlockSpec((tm, 3 * d_na), lambda i: (i, 0)),
            pl.BlockSpec((d_na, tm), lambda i: (0, i)),
            pl.BlockSpec((tm, d_kv), lambda i: (i, 0)),
            pl.BlockSpec((N_KV_HEADS, 1, HEAD_DIM, tm), lambda i: (0, i, 0, 0)),
        ],
        out_shape=[
            jax.ShapeDtypeStruct((s, 3 * d_na), BF16),
            jax.ShapeDtypeStruct((d_na, s), BF16),
            jax.ShapeDtypeStruct((s, d_kv), BF16),
            jax.ShapeDtypeStruct((N_KV_HEADS, s // tm, HEAD_DIM, tm), BF16),
        ],
        scratch_shapes=[pltpu.VMEM((tm, d), BF16)],
        compiler_params=pltpu.CompilerParams(
            dimension_semantics=("parallel",), vmem_limit_bytes=VMEM_LIMIT),
        name="in_proj",
    )(x, w_in, cos, sin_lo, sin_hi, q_gain, k_gain)


def _na_kernel(idx_ref, q_ref, k_ref, v_ref, bias_ref, o_ref, *, blocks_per_step, n_rows):
    step = pl.program_id(1)
    n_blocks = n_rows // NA_G
    tq = NA_G * GRID_W
    slab = NA_SLAB * GRID_W
    pairs = NA_SLAB // 2

    def scores(i):
        b = step * blocks_per_step + i
        start_row = jnp.clip(b * NA_G - NA_KH // 2, 0, n_rows - NA_SLAB)
        start = pl.multiple_of(start_row * GRID_W, GRID_W)
        q = q_ref[i * tq:(i + 1) * tq, :]
        k = k_ref[pl.ds(start, slab), :]
        s = lax.dot_general(q, k, (((1,), (1,)), ((), ())), preferred_element_type=F32)
        pattern = jnp.where(b == 0, 0, jnp.where(b == n_blocks - 1, 2, 1))
        rows = []
        for rr in range(NA_G):
            tiles = [s[rr * GRID_W:(rr + 1) * GRID_W, up * 2 * GRID_W:(up + 1) * 2 * GRID_W]
                     + bias_ref[idx_ref[(pattern * NA_G + rr) * pairs + up]] for up in range(pairs)]
            rows.append(jnp.concatenate(tiles, axis=1))
        return jnp.concatenate(rows, axis=0), start

    nxt = scores(0)
    for i in range(blocks_per_step):
        s, start = nxt
        if i + 1 < blocks_per_step:
            nxt = scores(i + 1)
        m = jnp.max(s, axis=-1, keepdims=True)
        p = jnp.exp2(s - m)
        l = jnp.sum(p, axis=-1, keepdims=True)
        v = v_ref[pl.ds(start, slab), :]
        o = jnp.dot(p.astype(BF16), v, preferred_element_type=F32) / l
        o_ref[i * tq:(i + 1) * tq, :] = o.astype(o_ref.dtype)


def _na_attn(h, bias, bias_idx, layer, n_heads):
    s = h.shape[0]
    n_rows = s // GRID_W
    blocks_per_step = NA_BLOCKS
    rows_per_step = blocks_per_step * NA_G
    assert n_rows % rows_per_step == 0 and n_rows >= NA_SLAB
    tq = rows_per_step * GRID_W
    n_tiles = bias.shape[2]
    return pl.pallas_call(
        functools.partial(_na_kernel, blocks_per_step=blocks_per_step, n_rows=n_rows),
        grid=(n_heads, n_rows // rows_per_step),
        in_specs=[
            pl.BlockSpec(memory_space=pltpu.SMEM),
            pl.BlockSpec((tq, HEAD_DIM), lambda hd, rb: (rb, hd)),
            pl.BlockSpec((s, HEAD_DIM), lambda hd, rb: (0, n_heads + hd)),
            pl.BlockSpec((s, HEAD_DIM), lambda hd, rb: (0, 2 * n_heads + hd)),
            pl.BlockSpec((None, None, n_tiles, GRID_W, 2 * GRID_W), lambda hd, rb: (layer, hd, 0, 0, 0)),
        ],
        out_specs=pl.BlockSpec((tq, HEAD_DIM), lambda hd, rb: (rb, hd)),
        out_shape=jax.ShapeDtypeStruct((s, n_heads * HEAD_DIM), BF16),
        compiler_params=pltpu.CompilerParams(
            dimension_semantics=("parallel", "arbitrary"), vmem_limit_bytes=VMEM_LIMIT),
        name="na_attn",
    )(bias_idx, h, h, h, bias)


def _na_bias_tiles(rel_bias):
    w, kw = GRID_W, NA_KW
    nl, nh, nd, _ = rel_bias.shape
    rb = rel_bias.astype(F32) * LOG2E
    ring = jnp.concatenate([
        rb[..., kw - 1:],
        jnp.repeat(rb[..., -1:], w - kw + 1, axis=-1),
        jnp.repeat(rb[..., :1], w - kw + 1, axis=-1),
        rb[..., 1:kw - 1],
    ], axis=-1)
    toep = jnp.tile(ring, (1, 1, 1, w))[..., :w * (2 * w - 1)]
    toep = toep.reshape(nl, nh, nd, w, 2 * w - 1)[..., :w]
    c = jnp.arange(w)
    c0 = jnp.clip(c - kw // 2, 0, w - kw)
    col_in = (c[None, :] >= c0[:, None]) & (c[None, :] < c0[:, None] + kw)
    toep = jnp.where(col_in, toep, MASK_VALUE)
    masked = jnp.full((nl, nh, 1, w, w), MASK_VALUE, F32)
    ext = jnp.concatenate([masked, toep, masked], axis=2)
    first, second = ext[:, :, :-1], ext[:, :, 1:]
    blank = jnp.broadcast_to(masked, first.shape)
    return jnp.concatenate([jnp.concatenate([first, second], axis=-1),
                            jnp.concatenate([blank, second], axis=-1),
                            jnp.concatenate([first, blank], axis=-1)], axis=2)


def _na_bias_index(n_rows):
    n_var = 2 * NA_KH
    first, interior, last = (0, 0), (1, -(NA_KH // 2)), (n_rows // NA_G - 1, NA_G - NA_SLAB)
    idx = []
    for b, rel in (first, interior, last):
        for rr in range(NA_G):
            r = b * NA_G + rr
            lo = min(max(r - NA_KH // 2, 0), n_rows - NA_KH) - (b * NA_G + rel)
            for u in range(0, NA_SLAB, 2):
                has = [lo <= v < lo + NA_KH for v in (u, u + 1)]
                e = u + rel - rr + NA_KH
                if has[0] and has[1]:
                    idx.append(e)
                elif has[1]:
                    idx.append(n_var + e)
                elif has[0]:
                    idx.append(2 * n_var + e)
                else:
                    idx.append(2 * n_var)
    return jnp.asarray(idx, jnp.int32)


def _fold_row_groups(x, op):
    out = x[0:8]
    for r in range(8, x.shape[0], 8):
        out = op(out, x[r:r + 8])
    return out


def _gqa_kernel(qt_ref, k_ref, vt_ref, o_ref, sa_ref, sb_ref, *, tk, nk):
    tq = qt_ref.shape[1]
    qt = qt_ref[...]

    qblocks = [slice(n, n + GQA_QB) for n in range(0, tq, GQA_QB)]

    def scores(c, dst_ref):
        start = pl.multiple_of(c * tk, tk)
        k = k_ref[pl.ds(start, tk), :]
        m8 = []
        for qb in qblocks:
            st = jnp.dot(k, qt[:, qb], preferred_element_type=F32)
            dst_ref[:, qb] = st
            m8.append(_fold_row_groups(st, jnp.maximum))
        return jnp.concatenate(m8, axis=1)

    def softmax_step(c, src_ref, m8, m, l8, acc):
        m_new = jnp.maximum(m, jnp.max(m8, axis=0, keepdims=True))
        corr = jnp.exp2(m - m_new)
        vt = vt_ref[c]
        l8_new, acc_new = [], []
        for qb in qblocks:
            p = jnp.exp2(src_ref[:, qb] - m_new[:, qb])
            l8_new.append(corr[:, qb] * l8[:, qb] + _fold_row_groups(p, jnp.add))
            acc_new.append(corr[:, qb] * acc[:, qb]
                           + jnp.dot(vt, p.astype(BF16), preferred_element_type=F32))
        return m_new, jnp.concatenate(l8_new, axis=1), jnp.concatenate(acc_new, axis=1)

    def pair(c, m8a, m, l8, acc, last):
        m8b = scores(c + 1, sb_ref)
        m, l8, acc = softmax_step(c, sa_ref, m8a, m, l8, acc)
        if not last:
            m8a = scores(c + 2, sa_ref)
        m, l8, acc = softmax_step(c + 1, sb_ref, m8b, m, l8, acc)
        return m8a, m, l8, acc

    init = (scores(0, sa_ref), jnp.full((1, tq), -jnp.inf, F32), jnp.zeros((8, tq), F32),
            jnp.zeros((HEAD_DIM, tq), F32))
    carry = lax.fori_loop(0, nk // 2 - 1, lambda j, cr: pair(2 * j, *cr, last=False), init)
    _, _, l8, acc = pair(nk - 2, *carry, last=True)
    ot = acc / jnp.sum(l8, axis=0, keepdims=True)
    o_ref[...] = ot.T.astype(o_ref.dtype)


def _gqa_attn(qt, kg, vt):
    d_gqa, s = qt.shape
    n_heads = d_gqa // HEAD_DIM
    group = n_heads // N_KV_HEADS
    tq, tk = GQA_TQ, GQA_TK
    assert s % tq == 0 and s % (2 * tk) == 0
    nk = s // tk
    return pl.pallas_call(
        functools.partial(_gqa_kernel, tk=tk, nk=nk),
        grid=(n_heads, s // tq),
        in_specs=[
            pl.BlockSpec((HEAD_DIM, tq), lambda hd, i: (hd, i)),
            pl.BlockSpec((s, HEAD_DIM), lambda hd, i: (0, hd // group)),
            pl.BlockSpec((None, nk, HEAD_DIM, tk), lambda hd, i: (hd // group, 0, 0, 0)),
        ],
        out_specs=pl.BlockSpec((tq, HEAD_DIM), lambda hd, i: (i, hd)),
        out_shape=jax.ShapeDtypeStruct((s, d_gqa), BF16),
        scratch_shapes=[pltpu.VMEM((tk, tq), F32), pltpu.VMEM((tk, tq), F32)],
        compiler_params=pltpu.CompilerParams(
            dimension_semantics=("parallel", "arbitrary"), vmem_limit_bytes=VMEM_LIMIT),
        name="gqa_attn",
    )(qt, kg, vt)


def _out_proj_ln_kernel(ona_ref, og_ref, x_ref, w_ref, gna_ref, gg_ref, g_ref, b_ref, o_ref, *, alpha):
    tm, d_na = ona_ref.shape
    subs = [slice(r0, r0 + PROJ_SUB) for r0 in range(0, tm, PROJ_SUB)]
    normed = []
    for rows in subs:
        a = ona_ref[rows, :].astype(F32)
        b = og_ref[rows, :].astype(F32)
        normed.append(((a * _rms_scale(a) * gna_ref[...]).astype(BF16),
                       (b * _rms_scale(b) * gg_ref[...]).astype(BF16)))
    mixes = [jnp.dot(a, w_ref[:d_na, :], preferred_element_type=F32)
             + jnp.dot(b, w_ref[d_na:, :], preferred_element_type=F32) for a, b in normed]
    for rows, mix in zip(subs, mixes):
        y = alpha * x_ref[rows, :] + mix
        o_ref[rows, :] = _layer_norm(y, g_ref[...], b_ref[...])


def _out_proj_ln(o_na, o_g, x, w_out, gn_na, gn_g, ln_g, ln_b, layer, alpha):
    s, d = x.shape
    d_na = o_na.shape[1]
    d_g = o_g.shape[1]
    tm = PROJ_TM
    return pl.pallas_call(
        functools.partial(_out_proj_ln_kernel, alpha=alpha),
        grid=(s // tm,),
        in_specs=[
            pl.BlockSpec((tm, d_na), lambda i: (i, 0)),
            pl.BlockSpec((tm, d_g), lambda i: (i, 0)),
            pl.BlockSpec((tm, d), lambda i: (i, 0)),
            pl.BlockSpec((None, d_na + d_g, d), lambda i: (layer, 0, 0)),
            pl.BlockSpec((None, 1, d_na), lambda i: (layer, 0, 0)),
            pl.BlockSpec((None, 1, d_g), lambda i: (layer, 0, 0)),
            pl.BlockSpec((None, 1, d), lambda i: (layer, 0, 0)),
            pl.BlockSpec((None, 1, d), lambda i: (layer, 0, 0)),
        ],
        out_specs=pl.BlockSpec((tm, d), lambda i: (i, 0)),
        out_shape=jax.ShapeDtypeStruct((s, d), F32),
        compiler_params=pltpu.CompilerParams(
            dimension_semantics=("parallel",), vmem_limit_bytes=VMEM_LIMIT),
        name="out_proj_ln",
    )(o_na, o_g, x, w_out, gn_na, gn_g, ln_g, ln_b)


def _rope_tables(s):
    half = HEAD_DIM // 2
    nfreq = half // 2
    t = jnp.arange(s)
    row = (t // GRID_W).astype(F32)
    col = (t % GRID_W).astype(F32)
    inv_freq = 1.0 / (ROPE_THETA ** (jnp.arange(nfreq, dtype=F32) / nfreq))
    lane = jnp.arange(HEAD_DIM)
    pos = jnp.where(lane[None, :] < half, row[:, None], col[:, None])
    ang = pos * inv_freq[lane % nfreq][None, :]
    cos, sin = jnp.cos(ang), jnp.sin(ang)
    lower = (lane % half) < nfreq
    sin_lo = jnp.where(lower[None, :], -sin, 0.0)
    sin_hi = jnp.where(lower[None, :], 0.0, sin)
    return cos, sin_lo, sin_hi


def kernel(x, ffn1_w_gate_up, ffn1_w_down, ln1_g, ln1_b, w_in, na_rel_bias, q_norm_g, k_norm_g,
           gn_na_g, gn_gqa_g, w_out, ln2_g, ln2_b, ffn2_w_gate_up, ffn2_w_down, ln3_g, ln3_b):
    batch, s, d = x.shape
    depth = w_in.shape[0]
    alpha = (2.0 * depth) ** 0.25
    d_na = d // 2
    n_heads_na = d_na // HEAD_DIM

    def row(p):
        return p.astype(F32)[:, None, :]

    wgu1, wd1 = ffn1_w_gate_up.astype(BF16), ffn1_w_down.astype(BF16)
    wgu2, wd2 = ffn2_w_gate_up.astype(BF16), ffn2_w_down.astype(BF16)
    w_in_b, w_out_b = w_in.astype(BF16), w_out.astype(BF16)
    cos, sin_lo, sin_hi = _rope_tables(s)
    bias = _na_bias_tiles(na_rel_bias)
    bias_idx = _na_bias_index(s // GRID_W)
    ln1g, ln1b, ln2g, ln2b, ln3g, ln3b = map(row, (ln1_g, ln1_b, ln2_g, ln2_b, ln3_g, ln3_b))
    qg, kg, gna, ggq = map(row, (q_norm_g, k_norm_g, gn_na_g, gn_gqa_g))

    outs = []
    for bi in range(batch):
        xs = x[bi]
        for layer in range(depth):
            xs = _ffn_ln(xs, wgu1, wd1, ln1g, ln1b, layer, alpha)
            h_na, q_t, k_g, v_t = _in_proj(xs, w_in_b, cos, sin_lo, sin_hi, qg, kg, layer)
            o_na = _na_attn(h_na, bias, bias_idx, layer, n_heads_na)
            o_g = _gqa_attn(q_t, k_g, v_t)
            xs = _out_proj_ln(o_na, o_g, xs, w_out_b, gna, ggq, ln2g, ln2b, layer, alpha)
            xs = _ffn_ln(xs, wgu2, wd2, ln3g, ln3b, layer, alpha)
        outs.append(xs)
    return outs[0][None] if batch == 1 else jnp.stack(outs, axis=0)
```

```python
import functools

import jax
import jax.numpy as jnp
from jax import lax
from jax.experimental import pallas as pl
from jax.experimental.pallas import tpu as pltpu

F32 = jnp.float32
BF16 = jnp.bfloat16

GRID_W = 64
HEAD_DIM = 128
N_KV_HEADS = 2
NA_KH = 8
NA_KW = 16
ROPE_THETA = 10000.0
RMS_EPS = 1e-6
LN_EPS = 1e-5
LOG2E = 1.4426950408889634
SM_SCALE = HEAD_DIM ** -0.5 * LOG2E
MASK_VALUE = -1e30

FFN_UP_TM = 2048
FFN_TF = 512
FFN_SUB = 512
FFN_DOWN_TM = 256
FFN_DOWN_SUB = 128
PROJ_TM = 512
PROJ_TN = 256
PROJ_SUB = 256
GQA_TQ = 2048
GQA_TK = 512
GQA_QB = 256
NA_G = 4
NA_SLAB = 12
NA_BLOCKS = 8
VMEM_LIMIT = 56 * 1024 * 1024


def _layer_norm(y, g, b):
    mu = jnp.mean(y, axis=-1, keepdims=True)
    yc = y - mu
    var = jnp.mean(yc * yc, axis=-1, keepdims=True)
    return yc * lax.rsqrt(var + LN_EPS) * g + b


def _rms_scale(t):
    return lax.rsqrt(jnp.mean(t * t, axis=-1, keepdims=True) + RMS_EPS)


def _ffn_up_kernel(xb_ref, wg_ref, wu_ref, h_ref):
    wg = wg_ref[...].astype(BF16)
    wu = wu_ref[...].astype(BF16)
    for r0 in range(0, xb_ref.shape[0], FFN_SUB):
        rows = slice(r0, r0 + FFN_SUB)
        xb = xb_ref[rows, :]
        gate = jnp.dot(xb, wg, preferred_element_type=F32)
        up = jnp.dot(xb, wu, preferred_element_type=F32)
        h_ref[rows, :] = (gate / (1.0 + jnp.exp(-gate)) * up).astype(h_ref.dtype)


def _ffn_up(xb, w_gu, layer):
    s, d = xb.shape
    d_ff = w_gu.shape[2] // 2
    tm, tf = FFN_UP_TM, FFN_TF
    assert s % tm == 0 and d_ff % tf == 0 and tm % FFN_SUB == 0
    nf = d_ff // tf
    return pl.pallas_call(
        _ffn_up_kernel,
        grid=(s // tm, nf),
        in_specs=[
            pl.BlockSpec((tm, d), lambda i, j: (i, 0)),
            pl.BlockSpec((None, d, tf), lambda i, j: (layer, 0, j)),
            pl.BlockSpec((None, d, tf), lambda i, j: (layer, 0, j + nf)),
        ],
        out_specs=pl.BlockSpec((tm, tf), lambda i, j: (i, j)),
        out_shape=jax.ShapeDtypeStruct((s, d_ff), BF16),
        compiler_params=pltpu.CompilerParams(
            dimension_semantics=("parallel", "arbitrary"), vmem_limit_bytes=VMEM_LIMIT),
        name="ffn_up",
    )(xb, w_gu, w_gu)


def _ffn_down_ln_kernel(h_ref, wd_ref, x_ref, g_ref, b_ref, o_ref, ob_ref, *, alpha):
    for r0 in range(0, x_ref.shape[0], FFN_DOWN_SUB):
        rows = slice(r0, r0 + FFN_DOWN_SUB)
        y = alpha * x_ref[rows, :] + 0.5 * jnp.dot(h_ref[rows, :], wd_ref[...], preferred_element_type=F32)
        out = _layer_norm(y, g_ref[...], b_ref[...])
        o_ref[rows, :] = out
        ob_ref[rows, :] = out.astype(ob_ref.dtype)


def _ffn_down_ln(h, w_d, x, ln_g, ln_b, layer, alpha):
    s, d = x.shape
    d_ff = h.shape[1]
    tm = FFN_DOWN_TM
    assert s % tm == 0 and tm % FFN_DOWN_SUB == 0
    return pl.pallas_call(
        functools.partial(_ffn_down_ln_kernel, alpha=alpha),
        grid=(s // tm,),
        in_specs=[
            pl.BlockSpec((tm, d_ff), lambda i: (i, 0)),
            pl.BlockSpec((None, d_ff, d), lambda i: (layer, 0, 0), pipeline_mode=pl.Buffered(1)),
            pl.BlockSpec((tm, d), lambda i: (i, 0)),
            pl.BlockSpec((None, 1, d), lambda i: (layer, 0, 0)),
            pl.BlockSpec((None, 1, d), lambda i: (layer, 0, 0)),
        ],
        out_specs=[pl.BlockSpec((tm, d), lambda i: (i, 0)), pl.BlockSpec((tm, d), lambda i: (i, 0))],
        out_shape=[jax.ShapeDtypeStruct((s, d), F32), jax.ShapeDtypeStruct((s, d), BF16)],
        compiler_params=pltpu.CompilerParams(
            dimension_semantics=("parallel",), vmem_limit_bytes=VMEM_LIMIT),
        name="ffn_down_ln",
    )(h, w_d, x, ln_g, ln_b)


def _rope(t, cos, sin_lo, sin_hi):
    return t * cos + pltpu.roll(t, 96, 1) * sin_lo + pltpu.roll(t, 32, 1) * sin_hi


def _in_proj_kernel(xb_ref, w_ref, cos_ref, slo_ref, shi_ref, qg_ref, kg_ref,
                    hna_ref, qt_ref, kgo_ref, vt_ref, *, d_na):
    tn = PROJ_TN
    heads = tn // HEAD_DIM

    def cols(c0):
        return jnp.dot(xb_ref[...], w_ref[:, c0:c0 + tn], preferred_element_type=F32)

    def head(acc, h):
        return acc[:, h * HEAD_DIM:(h + 1) * HEAD_DIM]

    def normed_rope(t, gain):
        t = t * _rms_scale(t) * gain
        return _rope(t, cos_ref[...], slo_ref[...], shi_ref[...])

    for c in range(0, d_na, tn):
        acc = cols(3 * d_na + c)
        for h in range(heads):
            t = normed_rope(head(acc, h), qg_ref[...]) * SM_SCALE
            qt_ref[c + h * HEAD_DIM:c + (h + 1) * HEAD_DIM, :] = t.T.astype(qt_ref.dtype)
    acc = cols(4 * d_na)
    for h in range(N_KV_HEADS):
        kgo_ref[:, h * HEAD_DIM:(h + 1) * HEAD_DIM] = normed_rope(head(acc, h), kg_ref[...]).astype(kgo_ref.dtype)
    acc = cols(4 * d_na + N_KV_HEADS * HEAD_DIM)
    for h in range(N_KV_HEADS):
        vt_ref[h, 0] = head(acc, h).T.astype(vt_ref.dtype)
    for c in range(0, d_na, tn):
        hna_ref[:, c:c + tn] = (cols(c) * SM_SCALE).astype(hna_ref.dtype)
    for c in range(d_na, 3 * d_na, tn):
        hna_ref[:, c:c + tn] = cols(c).astype(hna_ref.dtype)


def _in_proj(xb, w_in, cos, sin_lo, sin_hi, q_gain, k_gain, layer):
    s, d = xb.shape
    d_in = w_in.shape[2]
    tm, tn = PROJ_TM, PROJ_TN
    d_na = d // 2
    d_kv = N_KV_HEADS * HEAD_DIM
    assert d_na % tn == 0 and d_kv == tn and d_in == 4 * d_na + 2 * d_kv and tm == GQA_TK
    return pl.pallas_call(
        functools.partial(_in_proj_kernel, d_na=d_na),
        grid=(s // tm,),
        in_specs=[
            pl.BlockSpec((tm, d), lambda i: (i, 0)),
            pl.BlockSpec((None, d, d_in), lambda i: (layer, 0, 0), pipeline_mode=pl.Buffered(1)),
            pl.BlockSpec((tm, HEAD_DIM), lambda i: (i, 0)),
            pl.BlockSpec((tm, HEAD_DIM), lambda i: (i, 0)),
            pl.BlockSpec((tm, HEAD_DIM), lambda i: (i, 0)),
            pl.BlockSpec((None, 1, HEAD_DIM), lambda i: (layer, 0, 0)),
            pl.BlockSpec((None, 1, HEAD_DIM), lambda i: (layer, 0, 0)),
        ],
        out_specs=[
            pl.BlockSpec((tm, 3 * d_na), lambda i: (i, 0)),
            pl.BlockSpec((d_na, tm), lambda i: (0, i)),
            pl.BlockSpec((tm, d_kv), lambda i: (i, 0)),
            pl.BlockSpec((N_KV_HEADS, 1, HEAD_DIM, tm), lambda i: (0, i, 0, 0)),
        ],
        out_shape=[
            jax.ShapeDtypeStruct((s, 3 * d_na), BF16),
            jax.ShapeDtypeStruct((d_na, s), BF16),
            jax.ShapeDtypeStruct((s, d_kv), BF16),
            jax.ShapeDtypeStruct((N_KV_HEADS, s // tm, HEAD_DIM, tm), BF16),
        ],
        compiler_params=pltpu.CompilerParams(
            dimension_semantics=("parallel",), vmem_limit_bytes=VMEM_LIMIT),
        name="in_proj",
    )(xb, w_in, cos, sin_lo, sin_hi, q_gain, k_gain)


def _na_kernel(idx_ref, q_ref, k_ref, v_ref, bias_ref, o_ref, *, blocks_per_step, n_rows):
    step = pl.program_id(1)
    n_blocks = n_rows // NA_G
    tq = NA_G * GRID_W
    slab = NA_SLAB * GRID_W
    pairs = NA_SLAB // 2

    def scores(i):
        b = step * blocks_per_step + i
        start_row = jnp.clip(b * NA_G - NA_KH // 2, 0, n_rows - NA_SLAB)
        start = pl.multiple_of(start_row * GRID_W, GRID_W)
        q = q_ref[i * tq:(i + 1) * tq, :]
        k = k_ref[pl.ds(start, slab), :]
        s = lax.dot_general(q, k, (((1,), (1,)), ((), ())), preferred_element_type=F32)
        pattern = jnp.where(b == 0, 0, jnp.where(b == n_blocks - 1, 2, 1))
        rows = []
        for rr in range(NA_G):
            tiles = [s[rr * GRID_W:(rr + 1) * GRID_W, up * 2 * GRID_W:(up + 1) * 2 * GRID_W]
                     + bias_ref[idx_ref[(pattern * NA_G + rr) * pairs + up]] for up in range(pairs)]
            rows.append(jnp.concatenate(tiles, axis=1))
        return jnp.concatenate(rows, axis=0), start

    nxt = scores(0)
    for i in range(blocks_per_step):
        s, start = nxt
        if i + 1 < blocks_per_step:
            nxt = scores(i + 1)
        m = jnp.max(s, axis=-1, keepdims=True)
        p = jnp.exp2(s - m)
        l = jnp.sum(p, axis=-1, keepdims=True)
        v = v_ref[pl.ds(start, slab), :]
        o = jnp.dot(p.astype(BF16), v, preferred_element_type=F32) / l
        o_ref[i * tq:(i + 1) * tq, :] = o.astype(o_ref.dtype)


def _na_attn(h, bias, bias_idx, layer, n_heads):
    s = h.shape[0]
    n_rows = s // GRID_W
    blocks_per_step = NA_BLOCKS
    rows_per_step = blocks_per_step * NA_G
    assert n_rows % rows_per_step == 0 and n_rows >= NA_SLAB
    tq = rows_per_step * GRID_W
    n_tiles = bias.shape[2]
    return pl.pallas_call(
        functools.partial(_na_kernel, blocks_per_step=blocks_per_step, n_rows=n_rows),
        grid=(n_heads, n_rows // rows_per_step),
        in_specs=[
            pl.BlockSpec(memory_space=pltpu.SMEM),
            pl.BlockSpec((tq, HEAD_DIM), lambda hd, rb: (rb, hd)),
            pl.BlockSpec((s, HEAD_DIM), lambda hd, rb: (0, n_heads + hd)),
            pl.BlockSpec((s, HEAD_DIM), lambda hd, rb: (0, 2 * n_heads + hd)),
            pl.BlockSpec((None, None, n_tiles, GRID_W, 2 * GRID_W), lambda hd, rb: (layer, hd, 0, 0, 0)),
        ],
        out_specs=pl.BlockSpec((tq, HEAD_DIM), lambda hd, rb: (rb, hd)),
        out_shape=jax.ShapeDtypeStruct((s, n_heads * HEAD_DIM), BF16),
        compiler_params=pltpu.CompilerParams(
            dimension_semantics=("parallel", "arbitrary"), vmem_limit_bytes=VMEM_LIMIT),
        name="na_attn",
    )(bias_idx, h, h, h, bias)


def _na_bias_tiles(rel_bias):
    w, kw = GRID_W, NA_KW
    nl, nh, nd, _ = rel_bias.shape
    rb = rel_bias.astype(F32) * LOG2E
    ring = jnp.concatenate([
        rb[..., kw - 1:],
        jnp.repeat(rb[..., -1:], w - kw + 1, axis=-1),
        jnp.repeat(rb[..., :1], w - kw + 1, axis=-1),
        rb[..., 1:kw - 1],
    ], axis=-1)
    toep = jnp.tile(ring, (1, 1, 1, w))[..., :w * (2 * w - 1)]
    toep = toep.reshape(nl, nh, nd, w, 2 * w - 1)[..., :w]
    c = jnp.arange(w)
    c0 = jnp.clip(c - kw // 2, 0, w - kw)
    col_in = (c[None, :] >= c0[:, None]) & (c[None, :] < c0[:, None] + kw)
    toep = jnp.where(col_in, toep, MASK_VALUE)
    masked = jnp.full((nl, nh, 1, w, w), MASK_VALUE, F32)
    ext = jnp.concatenate([masked, toep, masked], axis=2)
    first, second = ext[:, :, :-1], ext[:, :, 1:]
    blank = jnp.broadcast_to(masked, first.shape)
    return jnp.concatenate([jnp.concatenate([first, second], axis=-1),
                            jnp.concatenate([blank, second], axis=-1),
                            jnp.concatenate([first, blank], axis=-1)], axis=2)


def _na_bias_index(n_rows):
    n_var = 2 * NA_KH
    first, interior, last = (0, 0), (1, -(NA_KH // 2)), (n_rows // NA_G - 1, NA_G - NA_SLAB)
    idx = []
    for b, rel in (first, interior, last):
        for rr in range(NA_G):
            r = b * NA_G + rr
            lo = min(max(r - NA_KH // 2, 0), n_rows - NA_KH) - (b * NA_G + rel)
            for u in range(0, NA_SLAB, 2):
                has = [lo <= v < lo + NA_KH for v in (u, u + 1)]
                e = u + rel - rr + NA_KH
                if has[0] and has[1]:
                    idx.append(e)
                elif has[1]:
                    idx.append(n_var + e)
                elif has[0]:
                    idx.append(2 * n_var + e)
                else:
                    idx.append(2 * n_var)
    return jnp.asarray(idx, jnp.int32)


def _fold_row_groups(x, op):
    out = x[0:8]
    for r in range(8, x.shape[0], 8):
        out = op(out, x[r:r + 8])
    return out


def _gqa_kernel(qt_ref, k_ref, vt_ref, o_ref, sa_ref, sb_ref, *, tk, nk):
    tq = qt_ref.shape[1]
    qt = qt_ref[...]

    qblocks = [slice(n, n + GQA_QB) for n in range(0, tq, GQA_QB)]

    def scores(c, dst_ref):
        start = pl.multiple_of(c * tk, tk)
        k = k_ref[pl.ds(start, tk), :]
        m8 = []
        for qb in qblocks:
            st = jnp.dot(k, qt[:, qb], preferred_element_type=F32)
            dst_ref[:, qb] = st
            m8.append(_fold_row_groups(st, jnp.maximum))
        return jnp.concatenate(m8, axis=1)

    def softmax_step(c, src_ref, m8, m, l8, acc):
        m_new = jnp.maximum(m, jnp.max(m8, axis=0, keepdims=True))
        corr = jnp.exp2(m - m_new)
        vt = vt_ref[c]
        l8_new, acc_new = [], []
        for qb in qblocks:
            p = jnp.exp2(src_ref[:, qb] - m_new[:, qb])
            l8_new.append(corr[:, qb] * l8[:, qb] + _fold_row_groups(p, jnp.add))
            acc_new.append(corr[:, qb] * acc[:, qb]
                           + jnp.dot(vt, p.astype(BF16), preferred_element_type=F32))
        return m_new, jnp.concatenate(l8_new, axis=1), jnp.concatenate(acc_new, axis=1)

    def pair(c, m8a, m, l8, acc, last):
        m8b = scores(c + 1, sb_ref)
        m, l8, acc = softmax_step(c, sa_ref, m8a, m, l8, acc)
        if not last:
            m8a = scores(c + 2, sa_ref)
        m, l8, acc = softmax_step(c + 1, sb_ref, m8b, m, l8, acc)
        return m8a, m, l8, acc

    init = (scores(0, sa_ref), jnp.full((1, tq), -jnp.inf, F32), jnp.zeros((8, tq), F32),
            jnp.zeros((HEAD_DIM, tq), F32))
    carry = lax.fori_loop(0, nk // 2 - 1, lambda j, cr: pair(2 * j, *cr, last=False), init)
    _, _, l8, acc = pair(nk - 2, *carry, last=True)
    ot = acc / jnp.sum(l8, axis=0, keepdims=True)
    o_ref[...] = ot.T.astype(o_ref.dtype)


def _gqa_attn(qt, kg, vt):
    d_gqa, s = qt.shape
    n_heads = d_gqa // HEAD_DIM
    group = n_heads // N_KV_HEADS
    tq, tk = GQA_TQ, GQA_TK
    assert s % tq == 0 and s % (2 * tk) == 0
    nk = s // tk
    return pl.pallas_call(
        functools.partial(_gqa_kernel, tk=tk, nk=nk),
        grid=(n_heads, s // tq),
        in_specs=[
            pl.BlockSpec((HEAD_DIM, tq), lambda hd, i: (hd, i)),
            pl.BlockSpec((s, HEAD_DIM), lambda hd, i: (0, hd // group)),
            pl.BlockSpec((None, nk, HEAD_DIM, tk), lambda hd, i: (hd // group, 0, 0, 0)),
        ],
        out_specs=pl.BlockSpec((tq, HEAD_DIM), lambda hd, i: (i, hd)),
        out_shape=jax.ShapeDtypeStruct((s, d_gqa), BF16),
        scratch_shapes=[pltpu.VMEM((tk, tq), F32), pltpu.VMEM((tk, tq), F32)],
        compiler_params=pltpu.CompilerParams(
            dimension_semantics=("parallel", "arbitrary"), vmem_limit_bytes=VMEM_LIMIT),
        name="gqa_attn",
    )(qt, kg, vt)


def _out_proj_ln_kernel(ona_ref, og_ref, x_ref, w_ref, gna_ref, gg_ref, g_ref, b_ref, o_ref, ob_ref, *, alpha):
    tm, d_na = ona_ref.shape
    subs = [slice(r0, r0 + PROJ_SUB) for r0 in range(0, tm, PROJ_SUB)]
    normed = []
    for rows in subs:
        a = ona_ref[rows, :].astype(F32)
        b = og_ref[rows, :].astype(F32)
        normed.append(((a * _rms_scale(a) * gna_ref[...]).astype(BF16),
                       (b * _rms_scale(b) * gg_ref[...]).astype(BF16)))
    mixes = [jnp.dot(a, w_ref[:d_na, :], preferred_element_type=F32)
             + jnp.dot(b, w_ref[d_na:, :], preferred_element_type=F32) for a, b in normed]
    for rows, mix in zip(subs, mixes):
        out = _layer_norm(alpha * x_ref[rows, :] + mix, g_ref[...], b_ref[...])
        o_ref[rows, :] = out
        ob_ref[rows, :] = out.astype(ob_ref.dtype)


def _out_proj_ln(o_na, o_g, x, w_out, gn_na, gn_g, ln_g, ln_b, layer, alpha):
    s, d = x.shape
    d_na = o_na.shape[1]
    d_g = o_g.shape[1]
    tm = PROJ_TM
    return pl.pallas_call(
        functools.partial(_out_proj_ln_kernel, alpha=alpha),
        grid=(s // tm,),
        in_specs=[
            pl.BlockSpec((tm, d_na), lambda i: (i, 0)),
            pl.BlockSpec((tm, d_g), lambda i: (i, 0)),
            pl.BlockSpec((tm, d), lambda i: (i, 0)),
            pl.BlockSpec((None, d_na + d_g, d), lambda i: (layer, 0, 0)),
            pl.BlockSpec((None, 1, d_na), lambda i: (layer, 0, 0)),
            pl.BlockSpec((None, 1, d_g), lambda i: (layer, 0, 0)),
            pl.BlockSpec((None, 1, d), lambda i: (layer, 0, 0)),
            pl.BlockSpec((None, 1, d), lambda i: (layer, 0, 0)),
        ],
        out_specs=[pl.BlockSpec((tm, d), lambda i: (i, 0)), pl.BlockSpec((tm, d), lambda i: (i, 0))],
        out_shape=[jax.ShapeDtypeStruct((s, d), F32), jax.ShapeDtypeStruct((s, d), BF16)],
        compiler_params=pltpu.CompilerParams(
            dimension_semantics=("parallel",), vmem_limit_bytes=VMEM_LIMIT),
        name="out_proj_ln",
    )(o_na, o_g, x, w_out, gn_na, gn_g, ln_g, ln_b)


def _rope_tables(s):
    half = HEAD_DIM // 2
    nfreq = half // 2
    t = jnp.arange(s)
    row = (t // GRID_W).astype(F32)
    col = (t % GRID_W).astype(F32)
    inv_freq = 1.0 / (ROPE_THETA ** (jnp.arange(nfreq, dtype=F32) / nfreq))
    lane = jnp.arange(HEAD_DIM)
    pos = jnp.where(lane[None, :] < half, row[:, None], col[:, None])
    ang = pos * inv_freq[lane % nfreq][None, :]
    cos, sin = jnp.cos(ang), jnp.sin(ang)
    lower = (lane % half) < nfreq
    sin_lo = jnp.where(lower[None, :], -sin, 0.0)
    sin_hi = jnp.where(lower[None, :], 0.0, sin)
    return cos, sin_lo, sin_hi


def kernel(x, ffn1_w_gate_up, ffn1_w_down, ln1_g, ln1_b, w_in, na_rel_bias, q_norm_g, k_norm_g,
           gn_na_g, gn_gqa_g, w_out, ln2_g, ln2_b, ffn2_w_gate_up, ffn2_w_down, ln3_g, ln3_b):
    batch, s, d = x.shape
    depth = w_in.shape[0]
    alpha = (2.0 * depth) ** 0.25
    d_na = d // 2
    n_heads_na = d_na // HEAD_DIM

    def row(p):
        return p.astype(F32)[:, None, :]

    wd1, wd2 = ffn1_w_down.astype(BF16), ffn2_w_down.astype(BF16)
    w_in_b, w_out_b = w_in.astype(BF16), w_out.astype(BF16)
    cos, sin_lo, sin_hi = _rope_tables(s)
    bias = _na_bias_tiles(na_rel_bias)
    bias_idx = _na_bias_index(s // GRID_W)
    ln1g, ln1b, ln2g, ln2b, ln3g, ln3b = map(row, (ln1_g, ln1_b, ln2_g, ln2_b, ln3_g, ln3_b))
    qg, kg, gna, ggq = map(row, (q_norm_g, k_norm_g, gn_na_g, gn_gqa_g))

    outs = []
    for bi in range(batch):
        xs = x[bi]
        xb = xs.astype(BF16)
        for layer in range(depth):
            hid = _ffn_up(xb, ffn1_w_gate_up, layer)
            xs, xb = _ffn_down_ln(hid, wd1, xs, ln1g, ln1b, layer, alpha)
            h_na, q_t, k_g, v_t = _in_proj(xb, w_in_b, cos, sin_lo, sin_hi, qg, kg, layer)
            o_na = _na_attn(h_na, bias, bias_idx, layer, n_heads_na)
            o_g = _gqa_attn(q_t, k_g, v_t)
            xs, xb = _out_proj_ln(o_na, o_g, xs, w_out_b, gna, ggq, ln2g, ln2b, layer, alpha)
            hid = _ffn_up(xb, ffn2_w_gate_up, layer)
            xs, xb = _ffn_down_ln(hid, wd2, xs, ln3g, ln3b, layer, alpha)
        outs.append(xs)
    return outs[0][None] if batch == 1 else jnp.stack(outs, axis=0)
```

```python
import functools

import jax
import jax.numpy as jnp
import numpy as np
from jax import lax
from jax.experimental import pallas as pl
from jax.experimental.pallas import tpu as pltpu

F32 = jnp.float32
BF16 = jnp.bfloat16

GRID_W = 64
HEAD_DIM = 128
N_KV_HEADS = 2
NA_KH = 8
NA_KW = 16
ROPE_THETA = 10000.0
RMS_EPS = 1e-6
LN_EPS = 1e-5
LOG2E = 1.4426950408889634
SM_SCALE = HEAD_DIM ** -0.5 * LOG2E
MASK_VALUE = -1e30

FFN_UP_TM = 2048
FFN_TF = 512
FFN_SUB = 512
FFN_DOWN_TM = 256
FFN_DOWN_SUB = 128
PROJ_TM = 512
PROJ_TN = 256
PROJ_SUB = 256
GQA_TQ = 2048
GQA_TK = 512
GQA_QB = 256
GQA_UNROLL = 4
NA_G = 4
NA_SLAB = 12
NA_BLOCKS = 8
VMEM_LIMIT = 56 * 1024 * 1024


def _layer_norm(y, g, b):
    mu = jnp.mean(y, axis=-1, keepdims=True)
    yc = y - mu
    var = jnp.mean(yc * yc, axis=-1, keepdims=True)
    return yc * lax.rsqrt(var + LN_EPS) * g + b


def _rms_scale(t):
    return lax.rsqrt(jnp.mean(t * t, axis=-1, keepdims=True) + RMS_EPS)


def _cast_specs(weights, layer, n_steps, step_of):
    in_specs, out_specs, out_shapes = [], [], []
    for w in weights:
        _, r, c = w.shape
        rb = r // n_steps
        assert rb * n_steps == r and rb % 16 == 0
        in_specs.append(pl.BlockSpec((None, rb, c), lambda *g: (layer, step_of(*g), 0)))
        out_specs.append(pl.BlockSpec((rb, c), lambda *g: (step_of(*g), 0)))
        out_shapes.append(jax.ShapeDtypeStruct((r, c), BF16))
    return in_specs, out_specs, out_shapes


def _ffn_up_kernel(xb_ref, wg_ref, wu_ref, *rest):
    n_cast = len(rest) // 2
    h_ref = rest[n_cast]
    for src, dst in zip(rest[:n_cast], rest[n_cast + 1:]):
        dst[...] = src[...].astype(dst.dtype)
    wg = wg_ref[...].astype(BF16)
    wu = wu_ref[...].astype(BF16)
    for r0 in range(0, xb_ref.shape[0], FFN_SUB):
        rows = slice(r0, r0 + FFN_SUB)
        xb = xb_ref[rows, :]
        gate = jnp.dot(xb, wg, preferred_element_type=F32)
        up = jnp.dot(xb, wu, preferred_element_type=F32)
        h_ref[rows, :] = (gate / (1.0 + jnp.exp(-gate)) * up).astype(h_ref.dtype)


def _ffn_up(xb, w_gu, layer, cast=()):
    s, d = xb.shape
    d_ff = w_gu.shape[2] // 2
    tm, tf = FFN_UP_TM, FFN_TF
    assert s % tm == 0 and d_ff % tf == 0 and tm % FFN_SUB == 0
    nf = d_ff // tf
    c_in, c_out, c_shapes = _cast_specs(cast, layer, (s // tm) * nf, lambda i, j: i * nf + j)
    return pl.pallas_call(
        _ffn_up_kernel,
        grid=(s // tm, nf),
        in_specs=[
            pl.BlockSpec((tm, d), lambda i, j: (i, 0)),
            pl.BlockSpec((None, d, tf), lambda i, j: (layer, 0, j)),
            pl.BlockSpec((None, d, tf), lambda i, j: (layer, 0, j + nf)),
        ] + c_in,
        out_specs=[pl.BlockSpec((tm, tf), lambda i, j: (i, j))] + c_out,
        out_shape=[jax.ShapeDtypeStruct((s, d_ff), BF16)] + c_shapes,
        compiler_params=pltpu.CompilerParams(
            dimension_semantics=("arbitrary", "arbitrary"), vmem_limit_bytes=VMEM_LIMIT),
        name="ffn_up",
    )(xb, w_gu, w_gu, *cast)


def _ffn_down_ln_kernel(h_ref, wd_ref, x_ref, g_ref, b_ref, *rest, alpha):
    n_cast = (len(rest) - 2) // 2
    o_ref, ob_ref = rest[n_cast], rest[n_cast + 1]
    for src, dst in zip(rest[:n_cast], rest[n_cast + 2:]):
        dst[...] = src[...].astype(dst.dtype)
    for r0 in range(0, x_ref.shape[0], FFN_DOWN_SUB):
        rows = slice(r0, r0 + FFN_DOWN_SUB)
        y = alpha * x_ref[rows, :] + 0.5 * jnp.dot(h_ref[rows, :], wd_ref[...], preferred_element_type=F32)
        out = _layer_norm(y, g_ref[...], b_ref[...])
        o_ref[rows, :] = out
        ob_ref[rows, :] = out.astype(ob_ref.dtype)


def _ffn_down_ln(h, w_d, x, ln_g, ln_b, layer, alpha, cast=()):
    s, d = x.shape
    d_ff = h.shape[1]
    tm = FFN_DOWN_TM
    assert s % tm == 0 and tm % FFN_DOWN_SUB == 0
    c_in, c_out, c_shapes = _cast_specs(cast, layer, s // tm, lambda i: i)
    return pl.pallas_call(
        functools.partial(_ffn_down_ln_kernel, alpha=alpha),
        grid=(s // tm,),
        in_specs=[
            pl.BlockSpec((tm, d_ff), lambda i: (i, 0)),
            pl.BlockSpec((d_ff, d), lambda i: (0, 0), pipeline_mode=pl.Buffered(1)),
            pl.BlockSpec((tm, d), lambda i: (i, 0)),
            pl.BlockSpec((None, 1, d), lambda i: (layer, 0, 0)),
            pl.BlockSpec((None, 1, d), lambda i: (layer, 0, 0)),
        ] + c_in,
        out_specs=[pl.BlockSpec((tm, d), lambda i: (i, 0)), pl.BlockSpec((tm, d), lambda i: (i, 0))] + c_out,
        out_shape=[jax.ShapeDtypeStruct((s, d), F32), jax.ShapeDtypeStruct((s, d), BF16)] + c_shapes,
        compiler_params=pltpu.CompilerParams(
            dimension_semantics=("arbitrary",), vmem_limit_bytes=VMEM_LIMIT),
        name="ffn_down_ln",
    )(h, w_d, x, ln_g, ln_b, *cast)


def _rope(t, cos, sin_lo, sin_hi):
    return t * cos + pltpu.roll(t, 96, 1) * sin_lo + pltpu.roll(t, 32, 1) * sin_hi


def _in_proj_kernel(xb_ref, w_ref, cos_ref, slo_ref, shi_ref, qg_ref, kg_ref,
                    hna_ref, qt_ref, kgo_ref, vt_ref, *, d_na):
    tn = PROJ_TN
    heads = tn // HEAD_DIM

    def cols(c0):
        return jnp.dot(xb_ref[...], w_ref[:, c0:c0 + tn], preferred_element_type=F32)

    def head(acc, h):
        return acc[:, h * HEAD_DIM:(h + 1) * HEAD_DIM]

    def normed_rope(t, gain):
        t = t * _rms_scale(t) * gain
        return _rope(t, cos_ref[...], slo_ref[...], shi_ref[...])

    for c in range(0, d_na, tn):
        acc = cols(3 * d_na + c)
        for h in range(heads):
            t = normed_rope(head(acc, h), qg_ref[...]) * SM_SCALE
            qt_ref[c + h * HEAD_DIM:c + (h + 1) * HEAD_DIM, :] = t.T.astype(qt_ref.dtype)
    acc = cols(4 * d_na)
    for h in range(N_KV_HEADS):
        kgo_ref[:, h * HEAD_DIM:(h + 1) * HEAD_DIM] = normed_rope(head(acc, h), kg_ref[...]).astype(kgo_ref.dtype)
    acc = cols(4 * d_na + N_KV_HEADS * HEAD_DIM)
    for h in range(N_KV_HEADS):
        vt_ref[h, 0] = head(acc, h).T.astype(vt_ref.dtype)
    for c in range(0, d_na, tn):
        hna_ref[:, c:c + tn] = (cols(c) * SM_SCALE).astype(hna_ref.dtype)
    for c in range(d_na, 3 * d_na, tn):
        hna_ref[:, c:c + tn] = cols(c).astype(hna_ref.dtype)


def _in_proj(xb, w_in, cos, sin_lo, sin_hi, q_gain, k_gain, layer):
    s, d = xb.shape
    d_in = w_in.shape[1]
    tm, tn = PROJ_TM, PROJ_TN
    d_na = d // 2
    d_kv = N_KV_HEADS * HEAD_DIM
    assert d_na % tn == 0 and d_kv == tn and d_in == 4 * d_na + 2 * d_kv and tm == GQA_TK
    return pl.pallas_call(
        functools.partial(_in_proj_kernel, d_na=d_na),
        grid=(s // tm,),
        in_specs=[
            pl.BlockSpec((tm, d), lambda i: (i, 0)),
            pl.BlockSpec((d, d_in), lambda i: (0, 0), pipeline_mode=pl.Buffered(1)),
            pl.BlockSpec((tm, HEAD_DIM), lambda i: (i, 0)),
            pl.BlockSpec((tm, HEAD_DIM), lambda i: (i, 0)),
            pl.BlockSpec((tm, HEAD_DIM), lambda i: (i, 0)),
            pl.BlockSpec((None, 1, HEAD_DIM), lambda i: (layer, 0, 0)),
            pl.BlockSpec((None, 1, HEAD_DIM), lambda i: (layer, 0, 0)),
        ],
        out_specs=[
            pl.BlockSpec((tm, 3 * d_na), lambda i: (i, 0)),
            pl.BlockSpec((d_na, tm), lambda i: (0, i)),
            pl.BlockSpec((tm, d_kv), lambda i: (i, 0)),
            pl.BlockSpec((N_KV_HEADS, 1, HEAD_DIM, tm), lambda i: (0, i, 0, 0)),
        ],
        out_shape=[
            jax.ShapeDtypeStruct((s, 3 * d_na), BF16),
            jax.ShapeDtypeStruct((d_na, s), BF16),
            jax.ShapeDtypeStruct((s, d_kv), BF16),
            jax.ShapeDtypeStruct((N_KV_HEADS, s // tm, HEAD_DIM, tm), BF16),
        ],
        compiler_params=pltpu.CompilerParams(
            dimension_semantics=("parallel",), vmem_limit_bytes=VMEM_LIMIT),
        name="in_proj",
    )(xb, w_in, cos, sin_lo, sin_hi, q_gain, k_gain)


def _na_kernel(idx_ref, q_ref, k_ref, v_ref, bias_ref, o_ref, *, blocks_per_step, n_rows):
    step = pl.program_id(1)
    n_blocks = n_rows // NA_G
    tq = NA_G * GRID_W
    slab = NA_SLAB * GRID_W
    pairs = NA_SLAB // 2

    def scores(i):
        b = step * blocks_per_step + i
        start_row = jnp.clip(b * NA_G - NA_KH // 2, 0, n_rows - NA_SLAB)
        start = pl.multiple_of(start_row * GRID_W, GRID_W)
        q = q_ref[i * tq:(i + 1) * tq, :]
        k = k_ref[pl.ds(start, slab), :]
        s = lax.dot_general(q, k, (((1,), (1,)), ((), ())), preferred_element_type=F32)
        pattern = jnp.where(b == 0, 0, jnp.where(b == n_blocks - 1, 2, 1))
        rows = []
        for rr in range(NA_G):
            tiles = [s[rr * GRID_W:(rr + 1) * GRID_W, up * 2 * GRID_W:(up + 1) * 2 * GRID_W]
                     + bias_ref[idx_ref[(pattern * NA_G + rr) * pairs + up]] for up in range(pairs)]
            rows.append(jnp.concatenate(tiles, axis=1))
        return jnp.concatenate(rows, axis=0), start

    nxt = scores(0)
    for i in range(blocks_per_step):
        s, start = nxt
        if i + 1 < blocks_per_step:
            nxt = scores(i + 1)
        m = jnp.max(s, axis=-1, keepdims=True)
        p = jnp.exp2(s - m)
        l = jnp.sum(p, axis=-1, keepdims=True)
        v = v_ref[pl.ds(start, slab), :]
        o = jnp.dot(p.astype(BF16), v, preferred_element_type=F32) / l
        o_ref[i * tq:(i + 1) * tq, :] = o.astype(o_ref.dtype)


def _na_attn(h, bias, bias_idx, layer, n_heads):
    s = h.shape[0]
    n_rows = s // GRID_W
    blocks_per_step = NA_BLOCKS
    rows_per_step = blocks_per_step * NA_G
    assert n_rows % rows_per_step == 0 and n_rows >= NA_SLAB
    tq = rows_per_step * GRID_W
    n_tiles = bias.shape[2]
    return pl.pallas_call(
        functools.partial(_na_kernel, blocks_per_step=blocks_per_step, n_rows=n_rows),
        grid=(n_heads, n_rows // rows_per_step),
        in_specs=[
            pl.BlockSpec(memory_space=pltpu.SMEM),
            pl.BlockSpec((tq, HEAD_DIM), lambda hd, rb: (rb, hd)),
            pl.BlockSpec((s, HEAD_DIM), lambda hd, rb: (0, n_heads + hd)),
            pl.BlockSpec((s, HEAD_DIM), lambda hd, rb: (0, 2 * n_heads + hd)),
            pl.BlockSpec((None, None, n_tiles, GRID_W, 2 * GRID_W), lambda hd, rb: (layer, hd, 0, 0, 0)),
        ],
        out_specs=pl.BlockSpec((tq, HEAD_DIM), lambda hd, rb: (rb, hd)),
        out_shape=jax.ShapeDtypeStruct((s, n_heads * HEAD_DIM), BF16),
        compiler_params=pltpu.CompilerParams(
            dimension_semantics=("parallel", "arbitrary"), vmem_limit_bytes=VMEM_LIMIT),
        name="na_attn",
    )(bias_idx, h, h, h, bias)


def _na_bias_tiles(rel_bias):
    w, kw = GRID_W, NA_KW
    nl, nh, nd, _ = rel_bias.shape
    rb = rel_bias.astype(F32) * LOG2E
    ring = jnp.concatenate([
        rb[..., kw - 1:],
        jnp.repeat(rb[..., -1:], w - kw + 1, axis=-1),
        jnp.repeat(rb[..., :1], w - kw + 1, axis=-1),
        rb[..., 1:kw - 1],
    ], axis=-1)
    toep = jnp.tile(ring, (1, 1, 1, w))[..., :w * (2 * w - 1)]
    toep = toep.reshape(nl, nh, nd, w, 2 * w - 1)[..., :w]
    c = jnp.arange(w)
    c0 = jnp.clip(c - kw // 2, 0, w - kw)
    col_in = (c[None, :] >= c0[:, None]) & (c[None, :] < c0[:, None] + kw)
    toep = jnp.where(col_in, toep, MASK_VALUE)
    masked = jnp.full((nl, nh, 1, w, w), MASK_VALUE, F32)
    ext = jnp.concatenate([masked, toep, masked], axis=2)
    first, second = ext[:, :, :-1], ext[:, :, 1:]
    blank = jnp.broadcast_to(masked, first.shape)
    return jnp.concatenate([jnp.concatenate([first, second], axis=-1),
                            jnp.concatenate([blank, second], axis=-1),
                            jnp.concatenate([first, blank], axis=-1)], axis=2)


def _na_bias_index(n_rows):
    n_var = 2 * NA_KH
    first, interior, last = (0, 0), (1, -(NA_KH // 2)), (n_rows // NA_G - 1, NA_G - NA_SLAB)
    idx = []
    for b, rel in (first, interior, last):
        for rr in range(NA_G):
            r = b * NA_G + rr
            lo = min(max(r - NA_KH // 2, 0), n_rows - NA_KH) - (b * NA_G + rel)
            for u in range(0, NA_SLAB, 2):
                has = [lo <= v < lo + NA_KH for v in (u, u + 1)]
                e = u + rel - rr + NA_KH
                if has[0] and has[1]:
                    idx.append(e)
                elif has[1]:
                    idx.append(n_var + e)
                elif has[0]:
                    idx.append(2 * n_var + e)
                else:
                    idx.append(2 * n_var)
    return jnp.asarray(idx, jnp.int32)


def _fold_row_groups(x, op):
    out = x[0:8]
    for r in range(8, x.shape[0], 8):
        out = op(out, x[r:r + 8])
    return out


def _gqa_kernel(qt_ref, k_ref, vt_ref, o_ref, sa_ref, sb_ref, *, tk, nk):
    tq = qt_ref.shape[1]
    qt = qt_ref[...]

    qblocks = [slice(n, n + GQA_QB) for n in range(0, tq, GQA_QB)]

    def scores(c, dst_ref):
        start = pl.multiple_of(c * tk, tk)
        k = k_ref[pl.ds(start, tk), :]
        m8 = []
        for qb in qblocks:
            st = jnp.dot(k, qt[:, qb], preferred_element_type=F32)
            dst_ref[:, qb] = st
            m8.append(_fold_row_groups(st, jnp.maximum))
        return jnp.concatenate(m8, axis=1)

    def softmax_step(c, src_ref, m8, m, l8, acc):
        m_new = jnp.maximum(m, jnp.max(m8, axis=0, keepdims=True))
        corr = jnp.exp2(m - m_new)
        vt = vt_ref[c]
        l8_new, acc_new = [], []
        for qb in qblocks:
            p = jnp.exp2(src_ref[:, qb] - m_new[:, qb])
            l8_new.append(corr[:, qb] * l8[:, qb] + _fold_row_groups(p, jnp.add))
            acc_new.append(corr[:, qb] * acc[:, qb]
                           + jnp.dot(vt, p.astype(BF16), preferred_element_type=F32))
        return m_new, jnp.concatenate(l8_new, axis=1), jnp.concatenate(acc_new, axis=1)

    def group(c0, m8, m, l8, acc, last):
        bufs = (sa_ref, sb_ref)
        for t in range(GQA_UNROLL):
            more = t + 1 < GQA_UNROLL or not last
            if more:
                m8_next = scores(c0 + t + 1, bufs[(t + 1) % 2])
            m, l8, acc = softmax_step(c0 + t, bufs[t % 2], m8, m, l8, acc)
            if more:
                m8 = m8_next
        return m8, m, l8, acc

    init = (scores(0, sa_ref), jnp.full((1, tq), -jnp.inf, F32), jnp.zeros((8, tq), F32),
            jnp.zeros((HEAD_DIM, tq), F32))
    carry = lax.fori_loop(0, nk // GQA_UNROLL - 1,
                          lambda j, cr: group(GQA_UNROLL * j, *cr, last=False), init)
    _, _, l8, acc = group(nk - GQA_UNROLL, *carry, last=True)
    ot = acc / jnp.sum(l8, axis=0, keepdims=True)
    o_ref[...] = ot.T.astype(o_ref.dtype)


def _gqa_attn(qt, kg, vt):
    d_gqa, s = qt.shape
    n_heads = d_gqa // HEAD_DIM
    group = n_heads // N_KV_HEADS
    tq, tk = GQA_TQ, GQA_TK
    assert s % tq == 0 and s % (GQA_UNROLL * tk) == 0 and GQA_UNROLL % 2 == 0
    nk = s // tk
    return pl.pallas_call(
        functools.partial(_gqa_kernel, tk=tk, nk=nk),
        grid=(n_heads, s // tq),
        in_specs=[
            pl.BlockSpec((HEAD_DIM, tq), lambda hd, i: (hd, i)),
            pl.BlockSpec((s, HEAD_DIM), lambda hd, i: (0, hd // group)),
            pl.BlockSpec((None, nk, HEAD_DIM, tk), lambda hd, i: (hd // group, 0, 0, 0)),
        ],
        out_specs=pl.BlockSpec((tq, HEAD_DIM), lambda hd, i: (i, hd)),
        out_shape=jax.ShapeDtypeStruct((s, d_gqa), BF16),
        scratch_shapes=[pltpu.VMEM((tk, tq), F32), pltpu.VMEM((tk, tq), F32)],
        compiler_params=pltpu.CompilerParams(
            dimension_semantics=("parallel", "arbitrary"), vmem_limit_bytes=VMEM_LIMIT),
        name="gqa_attn",
    )(qt, kg, vt)


def _out_proj_ln_kernel(ona_ref, og_ref, x_ref, w_ref, gna_ref, gg_ref, g_ref, b_ref, o_ref, ob_ref, *, alpha):
    for r0 in range(0, ona_ref.shape[0], PROJ_SUB):
        rows = slice(r0, r0 + PROJ_SUB)
        a = ona_ref[rows, :].astype(F32)
        b = og_ref[rows, :].astype(F32)
        mixed_in = jnp.concatenate([(a * _rms_scale(a) * gna_ref[...]).astype(BF16),
                                    (b * _rms_scale(b) * gg_ref[...]).astype(BF16)], axis=1)
        mix = jnp.dot(mixed_in, w_ref[...], preferred_element_type=F32)
        out = _layer_norm(alpha * x_ref[rows, :] + mix, g_ref[...], b_ref[...])
        o_ref[rows, :] = out
        ob_ref[rows, :] = out.astype(ob_ref.dtype)


def _out_proj_ln(o_na, o_g, x, w_out, gn_na, gn_g, ln_g, ln_b, layer, alpha):
    s, d = x.shape
    d_na = o_na.shape[1]
    d_g = o_g.shape[1]
    tm = PROJ_TM
    return pl.pallas_call(
        functools.partial(_out_proj_ln_kernel, alpha=alpha),
        grid=(s // tm,),
        in_specs=[
            pl.BlockSpec((tm, d_na), lambda i: (i, 0)),
            pl.BlockSpec((tm, d_g), lambda i: (i, 0)),
            pl.BlockSpec((tm, d), lambda i: (i, 0)),
            pl.BlockSpec((d_na + d_g, d), lambda i: (0, 0)),
            pl.BlockSpec((None, 1, d_na), lambda i: (layer, 0, 0)),
            pl.BlockSpec((None, 1, d_g), lambda i: (layer, 0, 0)),
            pl.BlockSpec((None, 1, d), lambda i: (layer, 0, 0)),
            pl.BlockSpec((None, 1, d), lambda i: (layer, 0, 0)),
        ],
        out_specs=[pl.BlockSpec((tm, d), lambda i: (i, 0)), pl.BlockSpec((tm, d), lambda i: (i, 0))],
        out_shape=[jax.ShapeDtypeStruct((s, d), F32), jax.ShapeDtypeStruct((s, d), BF16)],
        compiler_params=pltpu.CompilerParams(
            dimension_semantics=("parallel",), vmem_limit_bytes=VMEM_LIMIT),
        name="out_proj_ln",
    )(o_na, o_g, x, w_out, gn_na, gn_g, ln_g, ln_b)


def _rope_tables(s):
    half = HEAD_DIM // 2
    nfreq = half // 2
    t = np.arange(s)
    row = (t // GRID_W).astype(np.float32)
    col = (t % GRID_W).astype(np.float32)
    inv_freq = (1.0 / (np.float32(ROPE_THETA) ** (np.arange(nfreq, dtype=np.float32) / np.float32(nfreq)))
                ).astype(np.float32)
    lane = np.arange(HEAD_DIM)
    pos = np.where(lane[None, :] < half, row[:, None], col[:, None])
    ang = (pos * inv_freq[lane % nfreq][None, :]).astype(np.float32)
    cos, sin = np.cos(ang), np.sin(ang)
    lower = (lane % half) < nfreq
    sin_lo = np.where(lower[None, :], -sin, np.float32(0.0))
    sin_hi = np.where(lower[None, :], np.float32(0.0), sin)
    return tuple(jnp.asarray(a, F32) for a in (cos, sin_lo, sin_hi))


def kernel(x, ffn1_w_gate_up, ffn1_w_down, ln1_g, ln1_b, w_in, na_rel_bias, q_norm_g, k_norm_g,
           gn_na_g, gn_gqa_g, w_out, ln2_g, ln2_b, ffn2_w_gate_up, ffn2_w_down, ln3_g, ln3_b):
    batch, s, d = x.shape
    depth = w_in.shape[0]
    alpha = (2.0 * depth) ** 0.25
    d_na = d // 2
    n_heads_na = d_na // HEAD_DIM

    def row(p):
        return p.astype(F32)[:, None, :]

    cos, sin_lo, sin_hi = _rope_tables(s)
    bias = _na_bias_tiles(na_rel_bias)
    bias_idx = _na_bias_index(s // GRID_W)
    ln1g, ln1b, ln2g, ln2b, ln3g, ln3b = map(row, (ln1_g, ln1_b, ln2_g, ln2_b, ln3_g, ln3_b))
    qg, kg, gna, ggq = map(row, (q_norm_g, k_norm_g, gn_na_g, gn_gqa_g))

    outs = []
    for bi in range(batch):
        xs = x[bi]
        xb = xs.astype(BF16)
        for layer in range(depth):
            hid, wd_b = _ffn_up(xb, ffn1_w_gate_up, layer, cast=(ffn1_w_down,))
            xs, xb, w_in_b, w_out_b = _ffn_down_ln(hid, wd_b, xs, ln1g, ln1b, layer, alpha, cast=(w_in, w_out))
            h_na, q_t, k_g, v_t = _in_proj(xb, w_in_b, cos, sin_lo, sin_hi, qg, kg, layer)
            o_na = _na_attn(h_na, bias, bias_idx, layer, n_heads_na)
            o_g = _gqa_attn(q_t, k_g, v_t)
            xs, xb = _out_proj_ln(o_na, o_g, xs, w_out_b, gna, ggq, ln2g, ln2b, layer, alpha)
            hid, wd_b = _ffn_up(xb, ffn2_w_gate_up, layer, cast=(ffn2_w_down,))
            xs, xb = _ffn_down_ln(hid, wd_b, xs, ln3g, ln3b, layer, alpha)
        outs.append(xs)
    return outs[0][None] if batch == 1 else jnp.stack(outs, axis=0)
```

```python
import functools

import jax
import jax.numpy as jnp
import numpy as np
from jax import lax
from jax.experimental import pallas as pl
from jax.experimental.pallas import tpu as pltpu

F32 = jnp.float32
BF16 = jnp.bfloat16

GRID_W = 64
HEAD_DIM = 128
N_KV_HEADS = 2
NA_KH = 8
NA_KW = 16
ROPE_THETA = 10000.0
RMS_EPS = 1e-6
LN_EPS = 1e-5
LOG2E = 1.4426950408889634
SM_SCALE = HEAD_DIM ** -0.5 * LOG2E
MASK_VALUE = -1e30

FFN_UP_TM = 2048
FFN_TF = 512
FFN_SUB = 512
FFN_DOWN_TM = 256
FFN_DOWN_SUB = 128
PROJ_TM = 512
PROJ_TN = 256
PROJ_SUB = 256
GQA_TQ = 2048
GQA_TK = 512
GQA_QB = 256
GQA_UNROLL = 4
NA_G = 4
NA_SLAB = 12
NA_BLOCKS = 8
VMEM_LIMIT = 56 * 1024 * 1024


def _layer_norm(y, g, b):
    mu = jnp.mean(y, axis=-1, keepdims=True)
    yc = y - mu
    var = jnp.mean(yc * yc, axis=-1, keepdims=True)
    return yc * lax.rsqrt(var + LN_EPS) * g + b


def _rms_scale(t):
    return lax.rsqrt(jnp.mean(t * t, axis=-1, keepdims=True) + RMS_EPS)


def _cast_specs(weights, layer, n_steps, step_of):
    in_specs, out_specs, out_shapes = [], [], []
    for w in weights:
        _, r, c = w.shape
        rb = r // n_steps
        assert rb * n_steps == r and rb % 16 == 0
        in_specs.append(pl.BlockSpec((None, rb, c), lambda *g: (layer, step_of(*g), 0)))
        out_specs.append(pl.BlockSpec((rb, c), lambda *g: (step_of(*g), 0)))
        out_shapes.append(jax.ShapeDtypeStruct((r, c), BF16))
    return in_specs, out_specs, out_shapes


def _ffn_up_kernel(xb_ref, wg_ref, wu_ref, *rest):
    n_cast = len(rest) // 2
    h_ref = rest[n_cast]
    for src, dst in zip(rest[:n_cast], rest[n_cast + 1:]):
        dst[...] = src[...].astype(dst.dtype)
    wg = wg_ref[...].astype(BF16)
    wu = wu_ref[...].astype(BF16)
    for r0 in range(0, xb_ref.shape[0], FFN_SUB):
        rows = slice(r0, r0 + FFN_SUB)
        xb = xb_ref[rows, :]
        gate = jnp.dot(xb, wg, preferred_element_type=F32)
        up = jnp.dot(xb, wu, preferred_element_type=F32)
        h_ref[rows, :] = (gate / (1.0 + jnp.exp(-gate)) * up).astype(h_ref.dtype)


def _ffn_up(xb, w_gu, layer, cast=()):
    s, d = xb.shape
    d_ff = w_gu.shape[2] // 2
    tm, tf = FFN_UP_TM, FFN_TF
    assert s % tm == 0 and d_ff % tf == 0 and tm % FFN_SUB == 0
    nf = d_ff // tf
    c_in, c_out, c_shapes = _cast_specs(cast, layer, (s // tm) * nf, lambda i, j: i * nf + j)
    return pl.pallas_call(
        _ffn_up_kernel,
        grid=(s // tm, nf),
        in_specs=[
            pl.BlockSpec((tm, d), lambda i, j: (i, 0)),
            pl.BlockSpec((None, d, tf), lambda i, j: (layer, 0, j)),
            pl.BlockSpec((None, d, tf), lambda i, j: (layer, 0, j + nf)),
        ] + c_in,
        out_specs=[pl.BlockSpec((tm, tf), lambda i, j: (i, j))] + c_out,
        out_shape=[jax.ShapeDtypeStruct((s, d_ff), BF16)] + c_shapes,
        compiler_params=pltpu.CompilerParams(
            dimension_semantics=("arbitrary", "arbitrary"), vmem_limit_bytes=VMEM_LIMIT),
        name="ffn_up",
    )(xb, w_gu, w_gu, *cast)


def _ffn_down_ln_kernel(h_ref, wd_ref, x_ref, g_ref, b_ref, *rest, alpha):
    n_cast = (len(rest) - 2) // 2
    o_ref, ob_ref = rest[n_cast], rest[n_cast + 1]
    for src, dst in zip(rest[:n_cast], rest[n_cast + 2:]):
        dst[...] = src[...].astype(dst.dtype)
    for r0 in range(0, x_ref.shape[0], FFN_DOWN_SUB):
        rows = slice(r0, r0 + FFN_DOWN_SUB)
        y = alpha * x_ref[rows, :] + 0.5 * jnp.dot(h_ref[rows, :], wd_ref[...], preferred_element_type=F32)
        out = _layer_norm(y, g_ref[...], b_ref[...])
        o_ref[rows, :] = out
        ob_ref[rows, :] = out.astype(ob_ref.dtype)


def _ffn_down_ln(h, w_d, x, ln_g, ln_b, layer, alpha, cast=()):
    s, d = x.shape
    d_ff = h.shape[1]
    tm = FFN_DOWN_TM
    assert s % tm == 0 and tm % FFN_DOWN_SUB == 0
    c_in, c_out, c_shapes = _cast_specs(cast, layer, s // tm, lambda i: i)
    return pl.pallas_call(
        functools.partial(_ffn_down_ln_kernel, alpha=alpha),
        grid=(s // tm,),
        in_specs=[
            pl.BlockSpec((tm, d_ff), lambda i: (i, 0)),
            pl.BlockSpec((d_ff, d), lambda i: (0, 0), pipeline_mode=pl.Buffered(1)),
            pl.BlockSpec((tm, d), lambda i: (i, 0)),
            pl.BlockSpec((None, 1, d), lambda i: (layer, 0, 0)),
            pl.BlockSpec((None, 1, d), lambda i: (layer, 0, 0)),
        ] + c_in,
        out_specs=[pl.BlockSpec((tm, d), lambda i: (i, 0)), pl.BlockSpec((tm, d), lambda i: (i, 0))] + c_out,
        out_shape=[jax.ShapeDtypeStruct((s, d), F32), jax.ShapeDtypeStruct((s, d), BF16)] + c_shapes,
        compiler_params=pltpu.CompilerParams(
            dimension_semantics=("arbitrary",), vmem_limit_bytes=VMEM_LIMIT),
        name="ffn_down_ln",
    )(h, w_d, x, ln_g, ln_b, *cast)


def _rope(t, cos, sin_lo, sin_hi):
    return t * cos + pltpu.roll(t, 96, 1) * sin_lo + pltpu.roll(t, 32, 1) * sin_hi


def _in_proj_kernel(xb_ref, w_ref, cos_ref, slo_ref, shi_ref, qg_ref, kg_ref,
                    hna_ref, qt_ref, kgo_ref, vt_ref, *, d_na):
    tn = PROJ_TN
    heads = tn // HEAD_DIM

    def cols(c0):
        return jnp.dot(xb_ref[...], w_ref[:, c0:c0 + tn], preferred_element_type=F32)

    def head(acc, h):
        return acc[:, h * HEAD_DIM:(h + 1) * HEAD_DIM]

    def normed_rope(t, gain):
        t = t * _rms_scale(t) * gain
        return _rope(t, cos_ref[...], slo_ref[...], shi_ref[...])

    for c in range(0, d_na, tn):
        acc = cols(3 * d_na + c)
        for h in range(heads):
            t = normed_rope(head(acc, h), qg_ref[...]) * SM_SCALE
            qt_ref[c + h * HEAD_DIM:c + (h + 1) * HEAD_DIM, :] = t.T.astype(qt_ref.dtype)
    acc = cols(4 * d_na)
    for h in range(N_KV_HEADS):
        kgo_ref[:, h * HEAD_DIM:(h + 1) * HEAD_DIM] = normed_rope(head(acc, h), kg_ref[...]).astype(kgo_ref.dtype)
    acc = cols(4 * d_na + N_KV_HEADS * HEAD_DIM)
    for h in range(N_KV_HEADS):
        vt_ref[h, 0] = head(acc, h).T.astype(vt_ref.dtype)
    for c in range(0, d_na, tn):
        hna_ref[:, c:c + tn] = (cols(c) * SM_SCALE).astype(hna_ref.dtype)
    for c in range(d_na, 3 * d_na, tn):
        hna_ref[:, c:c + tn] = cols(c).astype(hna_ref.dtype)


def _in_proj(xb, w_in, cos, sin_lo, sin_hi, q_gain, k_gain, layer):
    s, d = xb.shape
    d_in = w_in.shape[1]
    tm, tn = PROJ_TM, PROJ_TN
    d_na = d // 2
    d_kv = N_KV_HEADS * HEAD_DIM
    assert d_na % tn == 0 and d_kv == tn and d_in == 4 * d_na + 2 * d_kv and tm == GQA_TK
    return pl.pallas_call(
        functools.partial(_in_proj_kernel, d_na=d_na),
        grid=(s // tm,),
        in_specs=[
            pl.BlockSpec((tm, d), lambda i: (i, 0)),
            pl.BlockSpec((d, d_in), lambda i: (0, 0), pipeline_mode=pl.Buffered(1)),
            pl.BlockSpec((tm, HEAD_DIM), lambda i: (i, 0)),
            pl.BlockSpec((tm, HEAD_DIM), lambda i: (i, 0)),
            pl.BlockSpec((tm, HEAD_DIM), lambda i: (i, 0)),
            pl.BlockSpec((None, 1, HEAD_DIM), lambda i: (layer, 0, 0)),
            pl.BlockSpec((None, 1, HEAD_DIM), lambda i: (layer, 0, 0)),
        ],
        out_specs=[
            pl.BlockSpec((tm, 3 * d_na), lambda i: (i, 0)),
            pl.BlockSpec((d_na, tm), lambda i: (0, i)),
            pl.BlockSpec((tm, d_kv), lambda i: (i, 0)),
            pl.BlockSpec((N_KV_HEADS, 1, HEAD_DIM, tm), lambda i: (0, i, 0, 0)),
        ],
        out_shape=[
            jax.ShapeDtypeStruct((s, 3 * d_na), BF16),
            jax.ShapeDtypeStruct((d_na, s), BF16),
            jax.ShapeDtypeStruct((s, d_kv), BF16),
            jax.ShapeDtypeStruct((N_KV_HEADS, s // tm, HEAD_DIM, tm), BF16),
        ],
        compiler_params=pltpu.CompilerParams(
            dimension_semantics=("parallel",), vmem_limit_bytes=VMEM_LIMIT),
        name="in_proj",
    )(xb, w_in, cos, sin_lo, sin_hi, q_gain, k_gain)


def _na_kernel(idx_ref, q_ref, k_ref, v_ref, bias_ref, o_ref, *, blocks_per_step, n_rows):
    step = pl.program_id(1)
    n_blocks = n_rows // NA_G
    tq = NA_G * GRID_W
    slab = NA_SLAB * GRID_W
    pairs = NA_SLAB // 2

    def scores(i):
        b = step * blocks_per_step + i
        start_row = jnp.clip(b * NA_G - NA_KH // 2, 0, n_rows - NA_SLAB)
        start = pl.multiple_of(start_row * GRID_W, GRID_W)
        q = q_ref[i * tq:(i + 1) * tq, :]
        k = k_ref[pl.ds(start, slab), :]
        s = lax.dot_general(q, k, (((1,), (1,)), ((), ())), preferred_element_type=F32)
        pattern = jnp.where(b == 0, 0, jnp.where(b == n_blocks - 1, 2, 1))
        rows = []
        for rr in range(NA_G):
            tiles = [s[rr * GRID_W:(rr + 1) * GRID_W, up * 2 * GRID_W:(up + 1) * 2 * GRID_W]
                     + bias_ref[idx_ref[(pattern * NA_G + rr) * pairs + up]] for up in range(pairs)]
            rows.append(jnp.concatenate(tiles, axis=1))
        return jnp.concatenate(rows, axis=0), start

    nxt = scores(0)
    for i in range(blocks_per_step):
        s, start = nxt
        if i + 1 < blocks_per_step:
            nxt = scores(i + 1)
        m = jnp.max(s, axis=-1, keepdims=True)
        p = jnp.exp2(s - m)
        l = jnp.sum(p, axis=-1, keepdims=True)
        v = v_ref[pl.ds(start, slab), :]
        o = jnp.dot(p.astype(BF16), v, preferred_element_type=F32) / l
        o_ref[i * tq:(i + 1) * tq, :] = o.astype(o_ref.dtype)


def _na_attn(h, bias, bias_idx, layer, n_heads):
    s = h.shape[0]
    n_rows = s // GRID_W
    blocks_per_step = NA_BLOCKS
    rows_per_step = blocks_per_step * NA_G
    assert n_rows % rows_per_step == 0 and n_rows >= NA_SLAB
    tq = rows_per_step * GRID_W
    n_tiles = bias.shape[2]
    return pl.pallas_call(
        functools.partial(_na_kernel, blocks_per_step=blocks_per_step, n_rows=n_rows),
        grid=(n_heads, n_rows // rows_per_step),
        in_specs=[
            pl.BlockSpec(memory_space=pltpu.SMEM),
            pl.BlockSpec((tq, HEAD_DIM), lambda hd, rb: (rb, hd)),
            pl.BlockSpec((s, HEAD_DIM), lambda hd, rb: (0, n_heads + hd)),
            pl.BlockSpec((s, HEAD_DIM), lambda hd, rb: (0, 2 * n_heads + hd)),
            pl.BlockSpec((None, None, n_tiles, GRID_W, 2 * GRID_W), lambda hd, rb: (layer, hd, 0, 0, 0)),
        ],
        out_specs=pl.BlockSpec((tq, HEAD_DIM), lambda hd, rb: (rb, hd)),
        out_shape=jax.ShapeDtypeStruct((s, n_heads * HEAD_DIM), BF16),
        compiler_params=pltpu.CompilerParams(
            dimension_semantics=("parallel", "arbitrary"), vmem_limit_bytes=VMEM_LIMIT),
        name="na_attn",
    )(bias_idx, h, h, h, bias)


def _na_bias_tiles(rel_bias):
    w, kw = GRID_W, NA_KW
    nl, nh, nd, _ = rel_bias.shape
    rb = rel_bias.astype(F32) * LOG2E
    ring = jnp.concatenate([
        rb[..., kw - 1:],
        jnp.repeat(rb[..., -1:], w - kw + 1, axis=-1),
        jnp.repeat(rb[..., :1], w - kw + 1, axis=-1),
        rb[..., 1:kw - 1],
    ], axis=-1)
    toep = jnp.tile(ring, (1, 1, 1, w))[..., :w * (2 * w - 1)]
    toep = toep.reshape(nl, nh, nd, w, 2 * w - 1)[..., :w]
    c = jnp.arange(w)
    c0 = jnp.clip(c - kw // 2, 0, w - kw)
    col_in = (c[None, :] >= c0[:, None]) & (c[None, :] < c0[:, None] + kw)
    toep = jnp.where(col_in, toep, MASK_VALUE)
    masked = jnp.full((nl, nh, 1, w, w), MASK_VALUE, F32)
    ext = jnp.concatenate([masked, toep, masked], axis=2)
    first, second = ext[:, :, :-1], ext[:, :, 1:]
    blank = jnp.broadcast_to(masked, first.shape)
    return jnp.concatenate([jnp.concatenate([first, second], axis=-1),
                            jnp.concatenate([blank, second], axis=-1),
                            jnp.concatenate([first, blank], axis=-1)], axis=2)


def _na_bias_index(n_rows):
    n_var = 2 * NA_KH
    first, interior, last = (0, 0), (1, -(NA_KH // 2)), (n_rows // NA_G - 1, NA_G - NA_SLAB)
    idx = []
    for b, rel in (first, interior, last):
        for rr in range(NA_G):
            r = b * NA_G + rr
            lo = min(max(r - NA_KH // 2, 0), n_rows - NA_KH) - (b * NA_G + rel)
            for u in range(0, NA_SLAB, 2):
                has = [lo <= v < lo + NA_KH for v in (u, u + 1)]
                e = u + rel - rr + NA_KH
                if has[0] and has[1]:
                    idx.append(e)
                elif has[1]:
                    idx.append(n_var + e)
                elif has[0]:
                    idx.append(2 * n_var + e)
                else:
                    idx.append(2 * n_var)
    return jnp.asarray(idx, jnp.int32)


def _fold_row_groups(x, op):
    out = x[0:8]
    for r in range(8, x.shape[0], 8):
        out = op(out, x[r:r + 8])
    return out


def _gqa_kernel(qt_ref, k_ref, vt_ref, o_ref, sa_ref, sb_ref, *, tk, nk):
    tq = qt_ref.shape[1]
    qt = qt_ref[...]

    qblocks = [slice(n, n + GQA_QB) for n in range(0, tq, GQA_QB)]

    def key_chunk(c):
        return k_ref[pl.ds(pl.multiple_of(c * tk, tk), tk), :]

    def scores_block(k, qb, dst_ref):
        st = jnp.dot(k, qt[:, qb], preferred_element_type=F32)
        dst_ref[:, qb] = st
        return _fold_row_groups(st, jnp.maximum)

    def scores(c, dst_ref):
        k = key_chunk(c)
        return jnp.concatenate([scores_block(k, qb, dst_ref) for qb in qblocks], axis=1)

    def step(c, src_ref, dst_ref, m8, m, l8, acc):
        m_new = jnp.maximum(m, jnp.max(m8, axis=0, keepdims=True))
        corr = jnp.exp2(m - m_new)
        vt = vt_ref[c]
        k_next = None if dst_ref is None else key_chunk(c + 1)
        m8_next, l8_new, acc_new = [], [], []
        for qb in qblocks:
            if dst_ref is not None:
                m8_next.append(scores_block(k_next, qb, dst_ref))
            p = jnp.exp2(src_ref[:, qb] - m_new[:, qb])
            l8_new.append(corr[:, qb] * l8[:, qb] + _fold_row_groups(p, jnp.add))
            acc_new.append(corr[:, qb] * acc[:, qb]
                           + jnp.dot(vt, p.astype(BF16), preferred_element_type=F32))
        m8 = jnp.concatenate(m8_next, axis=1) if m8_next else m8
        return m8, m_new, jnp.concatenate(l8_new, axis=1), jnp.concatenate(acc_new, axis=1)

    def group(c0, m8, m, l8, acc, last):
        bufs = (sa_ref, sb_ref)
        for t in range(GQA_UNROLL):
            more = t + 1 < GQA_UNROLL or not last
            m8, m, l8, acc = step(c0 + t, bufs[t % 2], bufs[(t + 1) % 2] if more else None, m8, m, l8, acc)
        return m8, m, l8, acc

    init = (scores(0, sa_ref), jnp.full((1, tq), -jnp.inf, F32), jnp.zeros((8, tq), F32),
            jnp.zeros((HEAD_DIM, tq), F32))
    carry = lax.fori_loop(0, nk // GQA_UNROLL - 1,
                          lambda j, cr: group(GQA_UNROLL * j, *cr, last=False), init)
    _, _, l8, acc = group(nk - GQA_UNROLL, *carry, last=True)
    ot = acc / jnp.sum(l8, axis=0, keepdims=True)
    o_ref[...] = ot.T.astype(o_ref.dtype)


def _gqa_attn(qt, kg, vt):
    d_gqa, s = qt.shape
    n_heads = d_gqa // HEAD_DIM
    group = n_heads // N_KV_HEADS
    tq, tk = GQA_TQ, GQA_TK
    assert s % tq == 0 and s % (GQA_UNROLL * tk) == 0 and GQA_UNROLL % 2 == 0
    nk = s // tk
    return pl.pallas_call(
        functools.partial(_gqa_kernel, tk=tk, nk=nk),
        grid=(n_heads, s // tq),
        in_specs=[
            pl.BlockSpec((HEAD_DIM, tq), lambda hd, i: (hd, i)),
            pl.BlockSpec((s, HEAD_DIM), lambda hd, i: (0, hd // group)),
            pl.BlockSpec((None, nk, HEAD_DIM, tk), lambda hd, i: (hd // group, 0, 0, 0)),
        ],
        out_specs=pl.BlockSpec((tq, HEAD_DIM), lambda hd, i: (i, hd)),
        out_shape=jax.ShapeDtypeStruct((s, d_gqa), BF16),
        scratch_shapes=[pltpu.VMEM((tk, tq), F32), pltpu.VMEM((tk, tq), F32)],
        compiler_params=pltpu.CompilerParams(
            dimension_semantics=("parallel", "arbitrary"), vmem_limit_bytes=VMEM_LIMIT),
        name="gqa_attn",
    )(qt, kg, vt)


def _out_proj_ln_kernel(ona_ref, og_ref, x_ref, w_ref, gna_ref, gg_ref, g_ref, b_ref, o_ref, ob_ref, *, alpha):
    for r0 in range(0, ona_ref.shape[0], PROJ_SUB):
        rows = slice(r0, r0 + PROJ_SUB)
        a = ona_ref[rows, :].astype(F32)
        b = og_ref[rows, :].astype(F32)
        mixed_in = jnp.concatenate([(a * _rms_scale(a) * gna_ref[...]).astype(BF16),
                                    (b * _rms_scale(b) * gg_ref[...]).astype(BF16)], axis=1)
        mix = jnp.dot(mixed_in, w_ref[...], preferred_element_type=F32)
        out = _layer_norm(alpha * x_ref[rows, :] + mix, g_ref[...], b_ref[...])
        o_ref[rows, :] = out
        ob_ref[rows, :] = out.astype(ob_ref.dtype)


def _out_proj_ln(o_na, o_g, x, w_out, gn_na, gn_g, ln_g, ln_b, layer, alpha):
    s, d = x.shape
    d_na = o_na.shape[1]
    d_g = o_g.shape[1]
    tm = PROJ_TM
    return pl.pallas_call(
        functools.partial(_out_proj_ln_kernel, alpha=alpha),
        grid=(s // tm,),
        in_specs=[
            pl.BlockSpec((tm, d_na), lambda i: (i, 0)),
            pl.BlockSpec((tm, d_g), lambda i: (i, 0)),
            pl.BlockSpec((tm, d), lambda i: (i, 0)),
            pl.BlockSpec((d_na + d_g, d), lambda i: (0, 0)),
            pl.BlockSpec((None, 1, d_na), lambda i: (layer, 0, 0)),
            pl.BlockSpec((None, 1, d_g), lambda i: (layer, 0, 0)),
            pl.BlockSpec((None, 1, d), lambda i: (layer, 0, 0)),
            pl.BlockSpec((None, 1, d), lambda i: (layer, 0, 0)),
        ],
        out_specs=[pl.BlockSpec((tm, d), lambda i: (i, 0)), pl.BlockSpec((tm, d), lambda i: (i, 0))],
        out_shape=[jax.ShapeDtypeStruct((s, d), F32), jax.ShapeDtypeStruct((s, d), BF16)],
        compiler_params=pltpu.CompilerParams(
            dimension_semantics=("parallel",), vmem_limit_bytes=VMEM_LIMIT),
        name="out_proj_ln",
    )(o_na, o_g, x, w_out, gn_na, gn_g, ln_g, ln_b)


def _rope_tables(s):
    half = HEAD_DIM // 2
    nfreq = half // 2
    t = np.arange(s)
    row = (t // GRID_W).astype(np.float32)
    col = (t % GRID_W).astype(np.float32)
    inv_freq = (1.0 / (np.float32(ROPE_THETA) ** (np.arange(nfreq, dtype=np.float32) / np.float32(nfreq)))
                ).astype(np.float32)
    lane = np.arange(HEAD_DIM)
    pos = np.where(lane[None, :] < half, row[:, None], col[:, None])
    ang = (pos * inv_freq[lane % nfreq][None, :]).astype(np.float32)
    cos, sin = np.cos(ang), np.sin(ang)
    lower = (lane % half) < nfreq
    sin_lo = np.where(lower[None, :], -sin, np.float32(0.0))
    sin_hi = np.where(lower[None, :], np.float32(0.0), sin)
    return tuple(jnp.asarray(a, F32) for a in (cos, sin_lo, sin_hi))


def kernel(x, ffn1_w_gate_up, ffn1_w_down, ln1_g, ln1_b, w_in, na_rel_bias, q_norm_g, k_norm_g,
           gn_na_g, gn_gqa_g, w_out, ln2_g, ln2_b, ffn2_w_gate_up, ffn2_w_down, ln3_g, ln3_b):
    batch, s, d = x.shape
    depth = w_in.shape[0]
    alpha = (2.0 * depth) ** 0.25
    d_na = d // 2
    n_heads_na = d_na // HEAD_DIM

    def row(p):
        return p.astype(F32)[:, None, :]

    cos, sin_lo, sin_hi = _rope_tables(s)
    bias = _na_bias_tiles(na_rel_bias)
    bias_idx = _na_bias_index(s // GRID_W)
    ln1g, ln1b, ln2g, ln2b, ln3g, ln3b = map(row, (ln1_g, ln1_b, ln2_g, ln2_b, ln3_g, ln3_b))
    qg, kg, gna, ggq = map(row, (q_norm_g, k_norm_g, gn_na_g, gn_gqa_g))

    outs = []
    for bi in range(batch):
        xs = x[bi]
        xb = xs.astype(BF16)
        for layer in range(depth):
            hid, wd_b = _ffn_up(xb, ffn1_w_gate_up, layer, cast=(ffn1_w_down,))
            xs, xb, w_in_b, w_out_b = _ffn_down_ln(hid, wd_b, xs, ln1g, ln1b, layer, alpha, cast=(w_in, w_out))
            h_na, q_t, k_g, v_t = _in_proj(xb, w_in_b, cos, sin_lo, sin_hi, qg, kg, layer)
            o_na = _na_attn(h_na, bias, bias_idx, layer, n_heads_na)
            o_g = _gqa_attn(q_t, k_g, v_t)
            xs, xb = _out_proj_ln(o_na, o_g, xs, w_out_b, gna, ggq, ln2g, ln2b, layer, alpha)
            hid, wd_b = _ffn_up(xb, ffn2_w_gate_up, layer, cast=(ffn2_w_down,))
            xs, xb = _ffn_down_ln(hid, wd_b, xs, ln3g, ln3b, layer, alpha)
        outs.append(xs)
    return outs[0][None] if batch == 1 else jnp.stack(outs, axis=0)
```

```python
import functools

import jax
import jax.numpy as jnp
import numpy as np
from jax import lax
from jax.experimental import pallas as pl
from jax.experimental.pallas import tpu as pltpu

F32 = jnp.float32
BF16 = jnp.bfloat16

GRID_W = 64
HEAD_DIM = 128
N_KV_HEADS = 2
NA_KH = 8
NA_KW = 16
ROPE_THETA = 10000.0
RMS_EPS = 1e-6
LN_EPS = 1e-5
LOG2E = 1.4426950408889634
SM_SCALE = HEAD_DIM ** -0.5 * LOG2E
MASK_VALUE = -1e30

FFN_UP_TM = 2048
FFN_TF = 512
FFN_SUB = 512
FFN_DOWN_TM = 256
FFN_DOWN_SUB = 128
PROJ_TM = 512
PROJ_TN = 256
PROJ_SUB = 256
GQA_TQ = 2048
GQA_TK = 512
GQA_QB = 256
GQA_UNROLL = 4
NA_G = 4
NA_SLAB = 12
NA_BLOCKS = 16
NA_TOK = 256
NA_FOLD_CHAINS = 1
NA_AHEAD = 3
VMEM_LIMIT = 56 * 1024 * 1024


def _layer_norm(y, g, b):
    mu = jnp.mean(y, axis=-1, keepdims=True)
    yc = y - mu
    var = jnp.mean(yc * yc, axis=-1, keepdims=True)
    return yc * lax.rsqrt(var + LN_EPS) * g + b


def _rms_scale(t):
    return lax.rsqrt(jnp.mean(t * t, axis=-1, keepdims=True) + RMS_EPS)


def _fold_row_groups(x, op, chains=1):
    groups = [x[r:r + 8] for r in range(0, x.shape[0], 8)]
    outs = groups[:chains]
    for i, g in enumerate(groups[chains:]):
        outs[i % chains] = op(outs[i % chains], g)
    return functools.reduce(op, outs)


def _cast_specs(weights, layer, n_steps, step_of):
    in_specs, out_specs, out_shapes = [], [], []
    for w in weights:
        _, r, c = w.shape
        rb = r // n_steps
        assert rb * n_steps == r and rb % 16 == 0
        in_specs.append(pl.BlockSpec((None, rb, c), lambda *g: (layer, step_of(*g), 0)))
        out_specs.append(pl.BlockSpec((rb, c), lambda *g: (step_of(*g), 0)))
        out_shapes.append(jax.ShapeDtypeStruct((r, c), BF16))
    return in_specs, out_specs, out_shapes


def _ffn_up_kernel(xb_ref, wg_ref, wu_ref, *rest):
    n_cast = len(rest) // 2
    h_ref = rest[n_cast]
    for src, dst in zip(rest[:n_cast], rest[n_cast + 1:]):
        dst[...] = src[...].astype(dst.dtype)
    wg = wg_ref[...].astype(BF16)
    wu = wu_ref[...].astype(BF16)
    for r0 in range(0, xb_ref.shape[0], FFN_SUB):
        rows = slice(r0, r0 + FFN_SUB)
        xb = xb_ref[rows, :]
        gate = jnp.dot(xb, wg, preferred_element_type=F32)
        up = jnp.dot(xb, wu, preferred_element_type=F32)
        h_ref[rows, :] = (gate / (1.0 + jnp.exp(-gate)) * up).astype(h_ref.dtype)


def _ffn_up(xb, w_gu, layer, cast=()):
    s, d = xb.shape
    d_ff = w_gu.shape[2] // 2
    tm, tf = FFN_UP_TM, FFN_TF
    assert s % tm == 0 and d_ff % tf == 0 and tm % FFN_SUB == 0
    nf = d_ff // tf
    c_in, c_out, c_shapes = _cast_specs(cast, layer, (s // tm) * nf, lambda i, j: i * nf + j)
    return pl.pallas_call(
        _ffn_up_kernel,
        grid=(s // tm, nf),
        in_specs=[
            pl.BlockSpec((tm, d), lambda i, j: (i, 0)),
            pl.BlockSpec((None, d, tf), lambda i, j: (layer, 0, j)),
            pl.BlockSpec((None, d, tf), lambda i, j: (layer, 0, j + nf)),
        ] + c_in,
        out_specs=[pl.BlockSpec((tm, tf), lambda i, j: (i, j))] + c_out,
        out_shape=[jax.ShapeDtypeStruct((s, d_ff), BF16)] + c_shapes,
        compiler_params=pltpu.CompilerParams(
            dimension_semantics=("arbitrary", "arbitrary"), vmem_limit_bytes=VMEM_LIMIT),
        name="ffn_up",
    )(xb, w_gu, w_gu, *cast)


def _ffn_down_ln_kernel(h_ref, wd_ref, x_ref, g_ref, b_ref, *rest, alpha):
    n_cast = (len(rest) - 2) // 2
    o_ref, ob_ref = rest[n_cast], rest[n_cast + 1]
    for src, dst in zip(rest[:n_cast], rest[n_cast + 2:]):
        dst[...] = src[...].astype(dst.dtype)
    for r0 in range(0, x_ref.shape[0], FFN_DOWN_SUB):
        rows = slice(r0, r0 + FFN_DOWN_SUB)
        y = alpha * x_ref[rows, :] + 0.5 * jnp.dot(h_ref[rows, :], wd_ref[...], preferred_element_type=F32)
        out = _layer_norm(y, g_ref[...], b_ref[...])
        o_ref[rows, :] = out
        ob_ref[rows, :] = out.astype(ob_ref.dtype)


def _ffn_down_ln(h, w_d, x, ln_g, ln_b, layer, alpha, cast=()):
    s, d = x.shape
    d_ff = h.shape[1]
    tm = FFN_DOWN_TM
    assert s % tm == 0 and tm % FFN_DOWN_SUB == 0
    c_in, c_out, c_shapes = _cast_specs(cast, layer, s // tm, lambda i: i)
    return pl.pallas_call(
        functools.partial(_ffn_down_ln_kernel, alpha=alpha),
        grid=(s // tm,),
        in_specs=[
            pl.BlockSpec((tm, d_ff), lambda i: (i, 0)),
            pl.BlockSpec((d_ff, d), lambda i: (0, 0), pipeline_mode=pl.Buffered(1)),
            pl.BlockSpec((tm, d), lambda i: (i, 0)),
            pl.BlockSpec((None, 1, d), lambda i: (layer, 0, 0)),
            pl.BlockSpec((None, 1, d), lambda i: (layer, 0, 0)),
        ] + c_in,
        out_specs=[pl.BlockSpec((tm, d), lambda i: (i, 0)), pl.BlockSpec((tm, d), lambda i: (i, 0))] + c_out,
        out_shape=[jax.ShapeDtypeStruct((s, d), F32), jax.ShapeDtypeStruct((s, d), BF16)] + c_shapes,
        compiler_params=pltpu.CompilerParams(
            dimension_semantics=("arbitrary",), vmem_limit_bytes=VMEM_LIMIT),
        name="ffn_down_ln",
    )(h, w_d, x, ln_g, ln_b, *cast)


def _rope(t, cos, sin_lo, sin_hi):
    return t * cos + pltpu.roll(t, 96, 1) * sin_lo + pltpu.roll(t, 32, 1) * sin_hi


def _in_proj_kernel(xb_ref, w_ref, cos_ref, slo_ref, shi_ref, qg_ref, kg_ref,
                    qnt_ref, kna_ref, vnt_ref, qt_ref, kgo_ref, vt_ref, *, d_na):
    tn = PROJ_TN
    heads = tn // HEAD_DIM
    tm = xb_ref.shape[0]

    def cols(c0):
        return jnp.dot(xb_ref[...], w_ref[:, c0:c0 + tn], preferred_element_type=F32)

    def head(acc, h):
        return acc[:, h * HEAD_DIM:(h + 1) * HEAD_DIM]

    def normed_rope(t, gain):
        t = t * _rms_scale(t) * gain
        return _rope(t, cos_ref[...], slo_ref[...], shi_ref[...])

    for c in range(0, d_na, tn):
        acc = cols(3 * d_na + c)
        for h in range(heads):
            t = normed_rope(head(acc, h), qg_ref[...]) * SM_SCALE
            qt_ref[c + h * HEAD_DIM:c + (h + 1) * HEAD_DIM, :] = t.T.astype(qt_ref.dtype)
    acc = cols(4 * d_na)
    for h in range(N_KV_HEADS):
        kgo_ref[:, h * HEAD_DIM:(h + 1) * HEAD_DIM] = normed_rope(head(acc, h), kg_ref[...]).astype(kgo_ref.dtype)
    acc = cols(4 * d_na + N_KV_HEADS * HEAD_DIM)
    for h in range(N_KV_HEADS):
        vt_ref[h, 0] = head(acc, h).T.astype(vt_ref.dtype)
    for c in range(0, d_na, tn):
        acc = cols(c) * SM_SCALE
        for h in range(heads):
            qnt_ref[c + h * HEAD_DIM:c + (h + 1) * HEAD_DIM, :] = head(acc, h).T.astype(qnt_ref.dtype)
    for c in range(0, d_na, tn):
        acc = cols(2 * d_na + c)
        for h in range(heads):
            vh = head(acc, h).T.astype(vnt_ref.dtype)
            for t0 in range(0, tm, NA_TOK):
                vnt_ref[c // HEAD_DIM + h, t0 // NA_TOK] = vh[:, t0:t0 + NA_TOK]
    for c in range(0, d_na, tn):
        kna_ref[:, c:c + tn] = cols(d_na + c).astype(kna_ref.dtype)


def _in_proj(xb, w_in, cos, sin_lo, sin_hi, q_gain, k_gain, layer):
    s, d = xb.shape
    d_in = w_in.shape[1]
    tm, tn = PROJ_TM, PROJ_TN
    d_na = d // 2
    n_heads_na = d_na // HEAD_DIM
    d_kv = N_KV_HEADS * HEAD_DIM
    assert d_na % tn == 0 and d_kv == tn and d_in == 4 * d_na + 2 * d_kv and tm == GQA_TK and tm % NA_TOK == 0
    return pl.pallas_call(
        functools.partial(_in_proj_kernel, d_na=d_na),
        grid=(s // tm,),
        in_specs=[
            pl.BlockSpec((tm, d), lambda i: (i, 0)),
            pl.BlockSpec((d, d_in), lambda i: (0, 0), pipeline_mode=pl.Buffered(1)),
            pl.BlockSpec((tm, HEAD_DIM), lambda i: (i, 0)),
            pl.BlockSpec((tm, HEAD_DIM), lambda i: (i, 0)),
            pl.BlockSpec((tm, HEAD_DIM), lambda i: (i, 0)),
            pl.BlockSpec((None, 1, HEAD_DIM), lambda i: (layer, 0, 0)),
            pl.BlockSpec((None, 1, HEAD_DIM), lambda i: (layer, 0, 0)),
        ],
        out_specs=[
            pl.BlockSpec((d_na, tm), lambda i: (0, i)),
            pl.BlockSpec((tm, d_na), lambda i: (i, 0)),
            pl.BlockSpec((n_heads_na, tm // NA_TOK, HEAD_DIM, NA_TOK), lambda i: (0, i, 0, 0)),
            pl.BlockSpec((d_na, tm), lambda i: (0, i)),
            pl.BlockSpec((tm, d_kv), lambda i: (i, 0)),
            pl.BlockSpec((N_KV_HEADS, 1, HEAD_DIM, tm), lambda i: (0, i, 0, 0)),
        ],
        out_shape=[
            jax.ShapeDtypeStruct((d_na, s), BF16),
            jax.ShapeDtypeStruct((s, d_na), BF16),
            jax.ShapeDtypeStruct((n_heads_na, s // NA_TOK, HEAD_DIM, NA_TOK), BF16),
            jax.ShapeDtypeStruct((d_na, s), BF16),
            jax.ShapeDtypeStruct((s, d_kv), BF16),
            jax.ShapeDtypeStruct((N_KV_HEADS, s // tm, HEAD_DIM, tm), BF16),
        ],
        compiler_params=pltpu.CompilerParams(
            dimension_semantics=("parallel",), vmem_limit_bytes=VMEM_LIMIT),
        name="in_proj",
    )(xb, w_in, cos, sin_lo, sin_hi, q_gain, k_gain)


def _na_kernel(idx_ref, qt_ref, k_ref, vt_ref, bias_ref, o_ref, *, blocks_per_step, n_rows):
    step = pl.program_id(1)
    n_blocks = n_rows // NA_G
    tq = NA_G * GRID_W
    slab = NA_SLAB * GRID_W
    pairs = NA_G // 2

    def scores(i):
        b = step * blocks_per_step + i
        start_row = jnp.clip(b * NA_G - NA_KH // 2, 0, n_rows - NA_SLAB)
        start = pl.multiple_of(start_row * GRID_W, NA_TOK)
        k = k_ref[pl.ds(start, slab), :]
        st = jnp.dot(k, qt_ref[:, i * tq:(i + 1) * tq], preferred_element_type=F32)
        pattern = jnp.where(b == 0, 0, jnp.where(b == n_blocks - 1, 2, 1))
        rows = []
        for u in range(NA_SLAB):
            tiles = [st[u * GRID_W:(u + 1) * GRID_W, j * 2 * GRID_W:(j + 1) * 2 * GRID_W]
                     + bias_ref[idx_ref[(pattern * NA_SLAB + u) * pairs + j]] for j in range(pairs)]
            rows.append(jnp.concatenate(tiles, axis=1))
        return jnp.concatenate(rows, axis=0), start_row // (NA_TOK // GRID_W)

    ahead = [scores(i) for i in range(min(NA_AHEAD, blocks_per_step))]
    for i in range(blocks_per_step):
        st, tok_blk = ahead.pop(0)
        if i + NA_AHEAD < blocks_per_step:
            ahead.append(scores(i + NA_AHEAD))
        m = jnp.max(_fold_row_groups(st, jnp.maximum, NA_FOLD_CHAINS), axis=0, keepdims=True)
        p = jnp.exp2(st - m)
        l = jnp.sum(_fold_row_groups(p, jnp.add, NA_FOLD_CHAINS), axis=0, keepdims=True)
        vt = jnp.concatenate([vt_ref[tok_blk + t] for t in range(slab // NA_TOK)], axis=1)
        ot = jnp.dot(vt, p.astype(BF16), preferred_element_type=F32) / l
        o_ref[i * tq:(i + 1) * tq, :] = ot.T.astype(o_ref.dtype)


def _na_attn(qnt, kna, vnt, bias, bias_idx, layer):
    d_na, s = qnt.shape
    n_heads = d_na // HEAD_DIM
    n_rows = s // GRID_W
    blocks_per_step = NA_BLOCKS
    rows_per_step = blocks_per_step * NA_G
    tok_rows = NA_TOK // GRID_W
    assert n_rows % rows_per_step == 0 and n_rows >= NA_SLAB and NA_G % 2 == 0
    assert NA_G % tok_rows == 0 and (NA_KH // 2) % tok_rows == 0 and NA_SLAB % tok_rows == 0
    tq = rows_per_step * GRID_W
    n_tiles = bias.shape[2]
    return pl.pallas_call(
        functools.partial(_na_kernel, blocks_per_step=blocks_per_step, n_rows=n_rows),
        grid=(n_heads, n_rows // rows_per_step),
        in_specs=[
            pl.BlockSpec(memory_space=pltpu.SMEM),
            pl.BlockSpec((HEAD_DIM, tq), lambda hd, rb: (hd, rb)),
            pl.BlockSpec((s, HEAD_DIM), lambda hd, rb: (0, hd)),
            pl.BlockSpec((None, s // NA_TOK, HEAD_DIM, NA_TOK), lambda hd, rb: (hd, 0, 0, 0)),
            pl.BlockSpec((None, None, n_tiles, GRID_W, 2 * GRID_W), lambda hd, rb: (layer, hd, 0, 0, 0)),
        ],
        out_specs=pl.BlockSpec((tq, HEAD_DIM), lambda hd, rb: (rb, hd)),
        out_shape=jax.ShapeDtypeStruct((s, d_na), BF16),
        compiler_params=pltpu.CompilerParams(
            dimension_semantics=("parallel", "arbitrary"), vmem_limit_bytes=VMEM_LIMIT),
        name="na_attn",
    )(bias_idx, qnt, kna, vnt, bias)


def _na_bias_tiles(rel_bias):
    w, kw = GRID_W, NA_KW
    nl, nh, nd, ne = rel_bias.shape
    c = np.arange(w)
    dcol = np.clip(c[:, None] - c[None, :], -(kw - 1), kw - 1) + (kw - 1)
    onehot = (dcol.reshape(-1)[None, :] == np.arange(ne)[:, None]).astype(np.float32)
    toep = jnp.dot(rel_bias.astype(F32).reshape(nl * nh * nd, ne) * LOG2E, jnp.asarray(onehot),
                   precision=lax.Precision.HIGHEST).reshape(nl, nh, nd, w, w)
    c0 = np.clip(c - kw // 2, 0, w - kw)
    col_in = (c[:, None] >= c0[None, :]) & (c[:, None] < c0[None, :] + kw)
    toep = jnp.where(jnp.asarray(col_in), toep, MASK_VALUE)
    masked = jnp.full((nl, nh, 1, w, w), MASK_VALUE, F32)
    ext = jnp.concatenate([masked, toep, masked], axis=2)
    first, second = ext[:, :, 1:], ext[:, :, :-1]
    blank = jnp.broadcast_to(masked, first.shape)
    return jnp.concatenate([jnp.concatenate([first, second], axis=-1),
                            jnp.concatenate([blank, second], axis=-1),
                            jnp.concatenate([first, blank], axis=-1)], axis=2)


def _na_bias_index(n_rows):
    n_var = 2 * NA_KH
    first, interior, last = (0, 0), (1, -(NA_KH // 2)), (n_rows // NA_G - 1, NA_G - NA_SLAB)
    idx = []
    for b, rel in (first, interior, last):
        lo = [min(max(b * NA_G + rr - NA_KH // 2, 0), n_rows - NA_KH) - (b * NA_G + rel) for rr in range(NA_G)]
        for u in range(NA_SLAB):
            for rr in range(0, NA_G, 2):
                has = [lo[r] <= u < lo[r] + NA_KH for r in (rr, rr + 1)]
                f = u + rel - rr + NA_KH - 1
                if has[0] and has[1]:
                    idx.append(f)
                elif has[1]:
                    idx.append(n_var + f)
                elif has[0]:
                    idx.append(2 * n_var + f)
                else:
                    idx.append(3 * n_var - 1)
    return jnp.asarray(idx, jnp.int32)


def _gqa_kernel(qt_ref, k_ref, vt_ref, o_ref, sa_ref, sb_ref, *, tk, nk):
    tq = qt_ref.shape[1]
    qt = qt_ref[...]

    qblocks = [slice(n, n + GQA_QB) for n in range(0, tq, GQA_QB)]

    def key_chunk(c):
        return k_ref[pl.ds(pl.multiple_of(c * tk, tk), tk), :]

    def scores_block(k, qb, dst_ref):
        st = jnp.dot(k, qt[:, qb], preferred_element_type=F32)
        dst_ref[:, qb] = st
        return _fold_row_groups(st, jnp.maximum)

    def scores(c, dst_ref):
        k = key_chunk(c)
        return jnp.concatenate([scores_block(k, qb, dst_ref) for qb in qblocks], axis=1)

    def step(c, src_ref, dst_ref, m8, m, l8, acc):
        m_new = jnp.maximum(m, jnp.max(m8, axis=0, keepdims=True))
        corr = jnp.exp2(m - m_new)
        vt = vt_ref[c]
        k_next = None if dst_ref is None else key_chunk(c + 1)
        m8_next, l8_new, acc_new = [], [], []
        for qb in qblocks:
            if dst_ref is not None:
                m8_next.append(scores_block(k_next, qb, dst_ref))
            p = jnp.exp2(src_ref[:, qb] - m_new[:, qb])
            l8_new.append(corr[:, qb] * l8[:, qb] + _fold_row_groups(p, jnp.add))
            acc_new.append(corr[:, qb] * acc[:, qb]
                           + jnp.dot(vt, p.astype(BF16), preferred_element_type=F32))
        m8 = jnp.concatenate(m8_next, axis=1) if m8_next else m8
        return m8, m_new, jnp.concatenate(l8_new, axis=1), jnp.concatenate(acc_new, axis=1)

    def group(c0, m8, m, l8, acc, last):
        bufs = (sa_ref, sb_ref)
        for t in range(GQA_UNROLL):
            more = t + 1 < GQA_UNROLL or not last
            m8, m, l8, acc = step(c0 + t, bufs[t % 2], bufs[(t + 1) % 2] if more else None, m8, m, l8, acc)
        return m8, m, l8, acc

    init = (scores(0, sa_ref), jnp.full((1, tq), -jnp.inf, F32), jnp.zeros((8, tq), F32),
            jnp.zeros((HEAD_DIM, tq), F32))
    carry = lax.fori_loop(0, nk // GQA_UNROLL - 1,
                          lambda j, cr: group(GQA_UNROLL * j, *cr, last=False), init)
    _, _, l8, acc = group(nk - GQA_UNROLL, *carry, last=True)
    ot = acc / jnp.sum(l8, axis=0, keepdims=True)
    o_ref[...] = ot.T.astype(o_ref.dtype)


def _gqa_attn(qt, kg, vt):
    d_gqa, s = qt.shape
    n_heads = d_gqa // HEAD_DIM
    group = n_heads // N_KV_HEADS
    tq, tk = GQA_TQ, GQA_TK
    assert s % tq == 0 and s % (GQA_UNROLL * tk) == 0 and GQA_UNROLL % 2 == 0
    nk = s // tk
    return pl.pallas_call(
        functools.partial(_gqa_kernel, tk=tk, nk=nk),
        grid=(n_heads, s // tq),
        in_specs=[
            pl.BlockSpec((HEAD_DIM, tq), lambda hd, i: (hd, i)),
            pl.BlockSpec((s, HEAD_DIM), lambda hd, i: (0, hd // group)),
            pl.BlockSpec((None, nk, HEAD_DIM, tk), lambda hd, i: (hd // group, 0, 0, 0)),
        ],
        out_specs=pl.BlockSpec((tq, HEAD_DIM), lambda hd, i: (i, hd)),
        out_shape=jax.ShapeDtypeStruct((s, d_gqa), BF16),
        scratch_shapes=[pltpu.VMEM((tk, tq), F32), pltpu.VMEM((tk, tq), F32)],
        compiler_params=pltpu.CompilerParams(
            dimension_semantics=("parallel", "arbitrary"), vmem_limit_bytes=VMEM_LIMIT),
        name="gqa_attn",
    )(qt, kg, vt)


def _out_proj_ln_kernel(ona_ref, og_ref, x_ref, w_ref, gna_ref, gg_ref, g_ref, b_ref, o_ref, ob_ref, *, alpha):
    for r0 in range(0, ona_ref.shape[0], PROJ_SUB):
        rows = slice(r0, r0 + PROJ_SUB)
        a = ona_ref[rows, :].astype(F32)
        b = og_ref[rows, :].astype(F32)
        mixed_in = jnp.concatenate([(a * _rms_scale(a) * gna_ref[...]).astype(BF16),
                                    (b * _rms_scale(b) * gg_ref[...]).astype(BF16)], axis=1)
        mix = jnp.dot(mixed_in, w_ref[...], preferred_element_type=F32)
        out = _layer_norm(alpha * x_ref[rows, :] + mix, g_ref[...], b_ref[...])
        o_ref[rows, :] = out
        ob_ref[rows, :] = out.astype(ob_ref.dtype)


def _out_proj_ln(o_na, o_g, x, w_out, gn_na, gn_g, ln_g, ln_b, layer, alpha):
    s, d = x.shape
    d_na = o_na.shape[1]
    d_g = o_g.shape[1]
    tm = PROJ_TM
    return pl.pallas_call(
        functools.partial(_out_proj_ln_kernel, alpha=alpha),
        grid=(s // tm,),
        in_specs=[
            pl.BlockSpec((tm, d_na), lambda i: (i, 0)),
            pl.BlockSpec((tm, d_g), lambda i: (i, 0)),
            pl.BlockSpec((tm, d), lambda i: (i, 0)),
            pl.BlockSpec((d_na + d_g, d), lambda i: (0, 0)),
            pl.BlockSpec((None, 1, d_na), lambda i: (layer, 0, 0)),
            pl.BlockSpec((None, 1, d_g), lambda i: (layer, 0, 0)),
            pl.BlockSpec((None, 1, d), lambda i: (layer, 0, 0)),
            pl.BlockSpec((None, 1, d), lambda i: (layer, 0, 0)),
        ],
        out_specs=[pl.BlockSpec((tm, d), lambda i: (i, 0)), pl.BlockSpec((tm, d), lambda i: (i, 0))],
        out_shape=[jax.ShapeDtypeStruct((s, d), F32), jax.ShapeDtypeStruct((s, d), BF16)],
        compiler_params=pltpu.CompilerParams(
            dimension_semantics=("parallel",), vmem_limit_bytes=VMEM_LIMIT),
        name="out_proj_ln",
    )(o_na, o_g, x, w_out, gn_na, gn_g, ln_g, ln_b)


def _rope_tables(s):
    half = HEAD_DIM // 2
    nfreq = half // 2
    t = np.arange(s)
    row = (t // GRID_W).astype(np.float32)
    col = (t % GRID_W).astype(np.float32)
    inv_freq = (1.0 / (np.float32(ROPE_THETA) ** (np.arange(nfreq, dtype=np.float32) / np.float32(nfreq)))
                ).astype(np.float32)
    lane = np.arange(HEAD_DIM)
    pos = np.where(lane[None, :] < half, row[:, None], col[:, None])
    ang = (pos * inv_freq[lane % nfreq][None, :]).astype(np.float32)
    cos, sin = np.cos(ang), np.sin(ang)
    lower = (lane % half) < nfreq
    sin_lo = np.where(lower[None, :], -sin, np.float32(0.0))
    sin_hi = np.where(lower[None, :], np.float32(0.0), sin)
    return tuple(jnp.asarray(a, F32) for a in (cos, sin_lo, sin_hi))


def kernel(x, ffn1_w_gate_up, ffn1_w_down, ln1_g, ln1_b, w_in, na_rel_bias, q_norm_g, k_norm_g,
           gn_na_g, gn_gqa_g, w_out, ln2_g, ln2_b, ffn2_w_gate_up, ffn2_w_down, ln3_g, ln3_b):
    batch, s, d = x.shape
    depth = w_in.shape[0]
    alpha = (2.0 * depth) ** 0.25

    def row(p):
        return p.astype(F32)[:, None, :]

    cos, sin_lo, sin_hi = _rope_tables(s)
    bias = _na_bias_tiles(na_rel_bias)
    bias_idx = _na_bias_index(s // GRID_W)
    ln1g, ln1b, ln2g, ln2b, ln3g, ln3b = map(row, (ln1_g, ln1_b, ln2_g, ln2_b, ln3_g, ln3_b))
    qg, kg, gna, ggq = map(row, (q_norm_g, k_norm_g, gn_na_g, gn_gqa_g))

    outs = []
    for bi in range(batch):
        xs = x[bi]
        xb = xs.astype(BF16)
        for layer in range(depth):
            hid, wd_b = _ffn_up(xb, ffn1_w_gate_up, layer, cast=(ffn1_w_down,))
            xs, xb, w_in_b, w_out_b = _ffn_down_ln(hid, wd_b, xs, ln1g, ln1b, layer, alpha, cast=(w_in, w_out))
            qn_t, k_n, vn_t, q_t, k_g, v_t = _in_proj(xb, w_in_b, cos, sin_lo, sin_hi, qg, kg, layer)
            o_na = _na_attn(qn_t, k_n, vn_t, bias, bias_idx, layer)
            o_g = _gqa_attn(q_t, k_g, v_t)
            xs, xb = _out_proj_ln(o_na, o_g, xs, w_out_b, gna, ggq, ln2g, ln2b, layer, alpha)
            hid, wd_b = _ffn_up(xb, ffn2_w_gate_up, layer, cast=(ffn2_w_down,))
            xs, xb = _ffn_down_ln(hid, wd_b, xs, ln3g, ln3b, layer, alpha)
        outs.append(xs)
    return outs[0][None] if batch == 1 else jnp.stack(outs, axis=0)
```

```python
import functools

import jax
import jax.numpy as jnp
import numpy as np
from jax import lax
from jax.experimental import pallas as pl
from jax.experimental.pallas import tpu as pltpu

F32 = jnp.float32
BF16 = jnp.bfloat16

GRID_W = 64
HEAD_DIM = 128
N_KV_HEADS = 2
NA_KH = 8
NA_KW = 16
ROPE_THETA = 10000.0
RMS_EPS = 1e-6
LN_EPS = 1e-5
LOG2E = 1.4426950408889634
SM_SCALE = HEAD_DIM ** -0.5 * LOG2E
MASK_VALUE = -1e30

FFN_UP_TM = 2048
FFN_TF = 512
FFN_SUB = 512
FFN_COLS = 256
FFN_DOWN_TM = 256
FFN_DOWN_SUB = 128
PROJ_TM = 512
PROJ_TN = 256
PROJ_SUB = 256
GQA_TQ = 2048
GQA_TK = 512
GQA_QB = 256
GQA_UNROLL = 4
NA_G = 4
NA_SLAB = 12
NA_BLOCKS = 16
NA_TOK = 256
NA_FOLD_CHAINS = 1
NA_AHEAD = 3
VMEM_LIMIT = 56 * 1024 * 1024


def _layer_norm(y, g, b):
    mu = jnp.mean(y, axis=-1, keepdims=True)
    yc = y - mu
    var = jnp.mean(yc * yc, axis=-1, keepdims=True)
    return yc * lax.rsqrt(var + LN_EPS) * g + b


def _rms_scale(t):
    return lax.rsqrt(jnp.mean(t * t, axis=-1, keepdims=True) + RMS_EPS)


def _fold_row_groups(x, op, chains=1):
    groups = [x[r:r + 8] for r in range(0, x.shape[0], 8)]
    outs = groups[:chains]
    for i, g in enumerate(groups[chains:]):
        outs[i % chains] = op(outs[i % chains], g)
    return functools.reduce(op, outs)


def _cast_specs(weights, layer, n_steps, step_of):
    in_specs, out_specs, out_shapes = [], [], []
    for w in weights:
        _, r, c = w.shape
        rb = r // n_steps
        assert rb * n_steps == r and rb % 16 == 0
        in_specs.append(pl.BlockSpec((None, rb, c), lambda *g: (layer, step_of(*g), 0)))
        out_specs.append(pl.BlockSpec((rb, c), lambda *g: (step_of(*g), 0)))
        out_shapes.append(jax.ShapeDtypeStruct((r, c), BF16))
    return in_specs, out_specs, out_shapes


def _ffn_up_kernel(xb_ref, wg_ref, wu_ref, *rest):
    n_cast = len(rest) // 2
    h_ref = rest[n_cast]
    for src, dst in zip(rest[:n_cast], rest[n_cast + 1:]):
        dst[...] = src[...].astype(dst.dtype)
    tm, tf = h_ref.shape
    col_slices = [slice(c0, c0 + FFN_COLS) for c0 in range(0, tf, FFN_COLS)]
    wg = [wg_ref[:, cs].astype(BF16) for cs in col_slices]
    wu = [wu_ref[:, cs].astype(BF16) for cs in col_slices]
    bounds = list(range(0, tm - FFN_SUB, FFN_SUB)) + [tm - FFN_SUB, tm - FFN_SUB // 2, tm]
    for r0, r1 in zip(bounds[:-1], bounds[1:]):
        xb = xb_ref[r0:r1, :]
        for cs, wgc, wuc in zip(col_slices, wg, wu):
            gate = jnp.dot(xb, wgc, preferred_element_type=F32)
            up = jnp.dot(xb, wuc, preferred_element_type=F32)
            h_ref[r0:r1, cs] = (gate / (1.0 + jnp.exp(-gate)) * up).astype(h_ref.dtype)


def _ffn_up(xb, w_gu, layer, cast=()):
    s, d = xb.shape
    d_ff = w_gu.shape[2] // 2
    tm, tf = FFN_UP_TM, FFN_TF
    assert s % tm == 0 and d_ff % tf == 0 and tm % FFN_SUB == 0
    nf = d_ff // tf
    c_in, c_out, c_shapes = _cast_specs(cast, layer, (s // tm) * nf, lambda i, j: i * nf + j)
    return pl.pallas_call(
        _ffn_up_kernel,
        grid=(s // tm, nf),
        in_specs=[
            pl.BlockSpec((tm, d), lambda i, j: (i, 0)),
            pl.BlockSpec((None, d, tf), lambda i, j: (layer, 0, j)),
            pl.BlockSpec((None, d, tf), lambda i, j: (layer, 0, j + nf)),
        ] + c_in,
        out_specs=[pl.BlockSpec((tm, tf), lambda i, j: (i, j))] + c_out,
        out_shape=[jax.ShapeDtypeStruct((s, d_ff), BF16)] + c_shapes,
        compiler_params=pltpu.CompilerParams(
            dimension_semantics=("arbitrary", "arbitrary"), vmem_limit_bytes=VMEM_LIMIT),
        name="ffn_up",
    )(xb, w_gu, w_gu, *cast)


def _ffn_down_ln_kernel(h_ref, wd_ref, x_ref, g_ref, b_ref, *rest, alpha):
    n_cast = (len(rest) - 2) // 2
    o_ref, ob_ref = rest[n_cast], rest[n_cast + 1]
    for src, dst in zip(rest[:n_cast], rest[n_cast + 2:]):
        dst[...] = src[...].astype(dst.dtype)
    for r0 in range(0, x_ref.shape[0], FFN_DOWN_SUB):
        rows = slice(r0, r0 + FFN_DOWN_SUB)
        y = alpha * x_ref[rows, :] + 0.5 * jnp.dot(h_ref[rows, :], wd_ref[...], preferred_element_type=F32)
        out = _layer_norm(y, g_ref[...], b_ref[...])
        o_ref[rows, :] = out
        ob_ref[rows, :] = out.astype(ob_ref.dtype)


def _ffn_down_ln(h, w_d, x, ln_g, ln_b, layer, alpha, cast=()):
    s, d = x.shape
    d_ff = h.shape[1]
    tm = FFN_DOWN_TM
    assert s % tm == 0 and tm % FFN_DOWN_SUB == 0
    c_in, c_out, c_shapes = _cast_specs(cast, layer, s // tm, lambda i: i)
    return pl.pallas_call(
        functools.partial(_ffn_down_ln_kernel, alpha=alpha),
        grid=(s // tm,),
        in_specs=[
            pl.BlockSpec((tm, d_ff), lambda i: (i, 0)),
            pl.BlockSpec((d_ff, d), lambda i: (0, 0), pipeline_mode=pl.Buffered(1)),
            pl.BlockSpec((tm, d), lambda i: (i, 0)),
            pl.BlockSpec((None, 1, d), lambda i: (layer, 0, 0)),
            pl.BlockSpec((None, 1, d), lambda i: (layer, 0, 0)),
        ] + c_in,
        out_specs=[pl.BlockSpec((tm, d), lambda i: (i, 0)), pl.BlockSpec((tm, d), lambda i: (i, 0))] + c_out,
        out_shape=[jax.ShapeDtypeStruct((s, d), F32), jax.ShapeDtypeStruct((s, d), BF16)] + c_shapes,
        compiler_params=pltpu.CompilerParams(
            dimension_semantics=("arbitrary",), vmem_limit_bytes=VMEM_LIMIT),
        name="ffn_down_ln",
    )(h, w_d, x, ln_g, ln_b, *cast)


def _rope(t, cos, sin_lo, sin_hi):
    return t * cos + pltpu.roll(t, 96, 1) * sin_lo + pltpu.roll(t, 32, 1) * sin_hi


def _in_proj_kernel(xb_ref, w_ref, cos_ref, slo_ref, shi_ref, qg_ref, kg_ref,
                    qnt_ref, kna_ref, vnt_ref, qt_ref, kgo_ref, vt_ref, *, d_na):
    tn = PROJ_TN
    heads = tn // HEAD_DIM
    tm = xb_ref.shape[0]

    def cols(c0):
        return jnp.dot(xb_ref[...], w_ref[:, c0:c0 + tn], preferred_element_type=F32)

    def head(acc, h):
        return acc[:, h * HEAD_DIM:(h + 1) * HEAD_DIM]

    def normed_rope(t, gain):
        t = t * _rms_scale(t) * gain
        return _rope(t, cos_ref[...], slo_ref[...], shi_ref[...])

    for c in range(0, d_na, tn):
        acc = cols(3 * d_na + c)
        for h in range(heads):
            t = normed_rope(head(acc, h), qg_ref[...]) * SM_SCALE
            qt_ref[c + h * HEAD_DIM:c + (h + 1) * HEAD_DIM, :] = t.T.astype(qt_ref.dtype)
    acc = cols(4 * d_na)
    for h in range(N_KV_HEADS):
        kgo_ref[:, h * HEAD_DIM:(h + 1) * HEAD_DIM] = normed_rope(head(acc, h), kg_ref[...]).astype(kgo_ref.dtype)
    acc = cols(4 * d_na + N_KV_HEADS * HEAD_DIM)
    for h in range(N_KV_HEADS):
        vt_ref[h, 0] = head(acc, h).T.astype(vt_ref.dtype)
    for c in range(0, d_na, tn):
        acc = cols(c) * SM_SCALE
        for h in range(heads):
            qnt_ref[c + h * HEAD_DIM:c + (h + 1) * HEAD_DIM, :] = head(acc, h).T.astype(qnt_ref.dtype)
    for c in range(0, d_na, tn):
        acc = cols(2 * d_na + c)
        for h in range(heads):
            vh = head(acc, h).T.astype(vnt_ref.dtype)
            for t0 in range(0, tm, NA_TOK):
                vnt_ref[c // HEAD_DIM + h, t0 // NA_TOK] = vh[:, t0:t0 + NA_TOK]
    for c in range(0, d_na, tn):
        kna_ref[:, c:c + tn] = cols(d_na + c).astype(kna_ref.dtype)


def _in_proj(xb, w_in, cos, sin_lo, sin_hi, q_gain, k_gain, layer):
    s, d = xb.shape
    d_in = w_in.shape[1]
    tm, tn = PROJ_TM, PROJ_TN
    d_na = d // 2
    n_heads_na = d_na // HEAD_DIM
    d_kv = N_KV_HEADS * HEAD_DIM
    assert d_na % tn == 0 and d_kv == tn and d_in == 4 * d_na + 2 * d_kv and tm == GQA_TK and tm % NA_TOK == 0
    return pl.pallas_call(
        functools.partial(_in_proj_kernel, d_na=d_na),
        grid=(s // tm,),
        in_specs=[
            pl.BlockSpec((tm, d), lambda i: (i, 0)),
            pl.BlockSpec((d, d_in), lambda i: (0, 0), pipeline_mode=pl.Buffered(1)),
            pl.BlockSpec((tm, HEAD_DIM), lambda i: (i, 0)),
            pl.BlockSpec((tm, HEAD_DIM), lambda i: (i, 0)),
            pl.BlockSpec((tm, HEAD_DIM), lambda i: (i, 0)),
            pl.BlockSpec((None, 1, HEAD_DIM), lambda i: (layer, 0, 0)),
            pl.BlockSpec((None, 1, HEAD_DIM), lambda i: (layer, 0, 0)),
        ],
        out_specs=[
            pl.BlockSpec((d_na, tm), lambda i: (0, i)),
            pl.BlockSpec((tm, d_na), lambda i: (i, 0)),
            pl.BlockSpec((n_heads_na, tm // NA_TOK, HEAD_DIM, NA_TOK), lambda i: (0, i, 0, 0)),
            pl.BlockSpec((d_na, tm), lambda i: (0, i)),
            pl.BlockSpec((tm, d_kv), lambda i: (i, 0)),
            pl.BlockSpec((N_KV_HEADS, 1, HEAD_DIM, tm), lambda i: (0, i, 0, 0)),
        ],
        out_shape=[
            jax.ShapeDtypeStruct((d_na, s), BF16),
            jax.ShapeDtypeStruct((s, d_na), BF16),
            jax.ShapeDtypeStruct((n_heads_na, s // NA_TOK, HEAD_DIM, NA_TOK), BF16),
            jax.ShapeDtypeStruct((d_na, s), BF16),
            jax.ShapeDtypeStruct((s, d_kv), BF16),
            jax.ShapeDtypeStruct((N_KV_HEADS, s // tm, HEAD_DIM, tm), BF16),
        ],
        compiler_params=pltpu.CompilerParams(
            dimension_semantics=("parallel",), vmem_limit_bytes=VMEM_LIMIT),
        name="in_proj",
    )(xb, w_in, cos, sin_lo, sin_hi, q_gain, k_gain)


def _na_kernel(idx_ref, qt_ref, k_ref, vt_ref, bias_ref, o_ref, *, blocks_per_step, n_rows):
    step = pl.program_id(1)
    n_blocks = n_rows // NA_G
    tq = NA_G * GRID_W
    slab = NA_SLAB * GRID_W
    pairs = NA_G // 2

    def scores(i):
        b = step * blocks_per_step + i
        start_row = jnp.clip(b * NA_G - NA_KH // 2, 0, n_rows - NA_SLAB)
        start = pl.multiple_of(start_row * GRID_W, NA_TOK)
        k = k_ref[pl.ds(start, slab), :]
        st = jnp.dot(k, qt_ref[:, i * tq:(i + 1) * tq], preferred_element_type=F32)
        pattern = jnp.where(b == 0, 0, jnp.where(b == n_blocks - 1, 2, 1))
        rows = []
        for u in range(NA_SLAB):
            tiles = [st[u * GRID_W:(u + 1) * GRID_W, j * 2 * GRID_W:(j + 1) * 2 * GRID_W]
                     + bias_ref[idx_ref[(pattern * NA_SLAB + u) * pairs + j]] for j in range(pairs)]
            rows.append(jnp.concatenate(tiles, axis=1))
        return jnp.concatenate(rows, axis=0), start_row // (NA_TOK // GRID_W)

    ahead = [scores(i) for i in range(min(NA_AHEAD, blocks_per_step))]
    for i in range(blocks_per_step):
        st, tok_blk = ahead.pop(0)
        if i + NA_AHEAD < blocks_per_step:
            ahead.append(scores(i + NA_AHEAD))
        m = jnp.max(_fold_row_groups(st, jnp.maximum, NA_FOLD_CHAINS), axis=0, keepdims=True)
        p = jnp.exp2(st - m)
        l = jnp.sum(_fold_row_groups(p, jnp.add, NA_FOLD_CHAINS), axis=0, keepdims=True)
        vt = jnp.concatenate([vt_ref[tok_blk + t] for t in range(slab // NA_TOK)], axis=1)
        ot = jnp.dot(vt, p.astype(BF16), preferred_element_type=F32) / l
        o_ref[i * tq:(i + 1) * tq, :] = ot.T.astype(o_ref.dtype)


def _na_attn(qnt, kna, vnt, bias, bias_idx, layer):
    d_na, s = qnt.shape
    n_heads = d_na // HEAD_DIM
    n_rows = s // GRID_W
    blocks_per_step = NA_BLOCKS
    rows_per_step = blocks_per_step * NA_G
    tok_rows = NA_TOK // GRID_W
    assert n_rows % rows_per_step == 0 and n_rows >= NA_SLAB and NA_G % 2 == 0
    assert NA_G % tok_rows == 0 and (NA_KH // 2) % tok_rows == 0 and NA_SLAB % tok_rows == 0
    tq = rows_per_step * GRID_W
    n_tiles = bias.shape[2]
    return pl.pallas_call(
        functools.partial(_na_kernel, blocks_per_step=blocks_per_step, n_rows=n_rows),
        grid=(n_heads, n_rows // rows_per_step),
        in_specs=[
            pl.BlockSpec(memory_space=pltpu.SMEM),
            pl.BlockSpec((HEAD_DIM, tq), lambda hd, rb: (hd, rb)),
            pl.BlockSpec((s, HEAD_DIM), lambda hd, rb: (0, hd)),
            pl.BlockSpec((None, s // NA_TOK, HEAD_DIM, NA_TOK), lambda hd, rb: (hd, 0, 0, 0)),
            pl.BlockSpec((None, None, n_tiles, GRID_W, 2 * GRID_W), lambda hd, rb: (layer, hd, 0, 0, 0)),
        ],
        out_specs=pl.BlockSpec((tq, HEAD_DIM), lambda hd, rb: (rb, hd)),
        out_shape=jax.ShapeDtypeStruct((s, d_na), BF16),
        compiler_params=pltpu.CompilerParams(
            dimension_semantics=("parallel", "arbitrary"), vmem_limit_bytes=VMEM_LIMIT),
        name="na_attn",
    )(bias_idx, qnt, kna, vnt, bias)


def _na_bias_tiles(rel_bias):
    w, kw = GRID_W, NA_KW
    nl, nh, nd, ne = rel_bias.shape
    c = np.arange(w)
    dcol = np.clip(c[:, None] - c[None, :], -(kw - 1), kw - 1) + (kw - 1)
    onehot = (dcol.reshape(-1)[None, :] == np.arange(ne)[:, None]).astype(np.float32)
    toep = jnp.dot(rel_bias.astype(F32).reshape(nl * nh * nd, ne) * LOG2E, jnp.asarray(onehot),
                   precision=lax.Precision.HIGHEST).reshape(nl, nh, nd, w, w)
    c0 = np.clip(c - kw // 2, 0, w - kw)
    col_in = (c[:, None] >= c0[None, :]) & (c[:, None] < c0[None, :] + kw)
    toep = jnp.where(jnp.asarray(col_in), toep, MASK_VALUE)
    masked = jnp.full((nl, nh, 1, w, w), MASK_VALUE, F32)
    ext = jnp.concatenate([masked, toep, masked], axis=2)
    first, second = ext[:, :, 1:], ext[:, :, :-1]
    blank = jnp.broadcast_to(masked, first.shape)
    return jnp.concatenate([jnp.concatenate([first, second], axis=-1),
                            jnp.concatenate([blank, second], axis=-1),
                            jnp.concatenate([first, blank], axis=-1)], axis=2)


def _na_bias_index(n_rows):
    n_var = 2 * NA_KH
    first, interior, last = (0, 0), (1, -(NA_KH // 2)), (n_rows // NA_G - 1, NA_G - NA_SLAB)
    idx = []
    for b, rel in (first, interior, last):
        lo = [min(max(b * NA_G + rr - NA_KH // 2, 0), n_rows - NA_KH) - (b * NA_G + rel) for rr in range(NA_G)]
        for u in range(NA_SLAB):
            for rr in range(0, NA_G, 2):
                has = [lo[r] <= u < lo[r] + NA_KH for r in (rr, rr + 1)]
                f = u + rel - rr + NA_KH - 1
                if has[0] and has[1]:
                    idx.append(f)
                elif has[1]:
                    idx.append(n_var + f)
                elif has[0]:
                    idx.append(2 * n_var + f)
                else:
                    idx.append(3 * n_var - 1)
    return jnp.asarray(idx, jnp.int32)


def _gqa_kernel(qt_ref, k_ref, vt_ref, o_ref, sa_ref, sb_ref, *, tk, nk):
    tq = qt_ref.shape[1]
    qt = qt_ref[...]

    qblocks = [slice(n, n + GQA_QB) for n in range(0, tq, GQA_QB)]

    def key_chunk(c):
        return k_ref[pl.ds(pl.multiple_of(c * tk, tk), tk), :]

    def scores_block(k, qb, dst_ref):
        st = jnp.dot(k, qt[:, qb], preferred_element_type=F32)
        dst_ref[:, qb] = st
        return _fold_row_groups(st, jnp.maximum)

    def scores(c, dst_ref):
        k = key_chunk(c)
        return jnp.concatenate([scores_block(k, qb, dst_ref) for qb in qblocks], axis=1)

    def step(c, src_ref, dst_ref, m8, m, l8, acc):
        m_new = jnp.maximum(m, jnp.max(m8, axis=0, keepdims=True))
        corr = jnp.exp2(m - m_new)
        vt = vt_ref[c]
        k_next = None if dst_ref is None else key_chunk(c + 1)
        m8_next, l8_new, acc_new = [], [], []
        for qb in qblocks:
            if dst_ref is not None:
                m8_next.append(scores_block(k_next, qb, dst_ref))
            p = jnp.exp2(src_ref[:, qb] - m_new[:, qb])
            l8_new.append(corr[:, qb] * l8[:, qb] + _fold_row_groups(p, jnp.add))
            acc_new.append(corr[:, qb] * acc[:, qb]
                           + jnp.dot(vt, p.astype(BF16), preferred_element_type=F32))
        m8 = jnp.concatenate(m8_next, axis=1) if m8_next else m8
        return m8, m_new, jnp.concatenate(l8_new, axis=1), jnp.concatenate(acc_new, axis=1)

    def group(c0, m8, m, l8, acc, last):
        bufs = (sa_ref, sb_ref)
        for t in range(GQA_UNROLL):
            more = t + 1 < GQA_UNROLL or not last
            m8, m, l8, acc = step(c0 + t, bufs[t % 2], bufs[(t + 1) % 2] if more else None, m8, m, l8, acc)
        return m8, m, l8, acc

    init = (scores(0, sa_ref), jnp.full((1, tq), -jnp.inf, F32), jnp.zeros((8, tq), F32),
            jnp.zeros((HEAD_DIM, tq), F32))
    carry = lax.fori_loop(0, nk // GQA_UNROLL - 1,
                          lambda j, cr: group(GQA_UNROLL * j, *cr, last=False), init)
    _, _, l8, acc = group(nk - GQA_UNROLL, *carry, last=True)
    ot = acc / jnp.sum(l8, axis=0, keepdims=True)
    o_ref[...] = ot.T.astype(o_ref.dtype)


def _gqa_attn(qt, kg, vt):
    d_gqa, s = qt.shape
    n_heads = d_gqa // HEAD_DIM
    group = n_heads // N_KV_HEADS
    tq, tk = GQA_TQ, GQA_TK
    assert s % tq == 0 and s % (GQA_UNROLL * tk) == 0 and GQA_UNROLL % 2 == 0
    nk = s // tk
    return pl.pallas_call(
        functools.partial(_gqa_kernel, tk=tk, nk=nk),
        grid=(n_heads, s // tq),
        in_specs=[
            pl.BlockSpec((HEAD_DIM, tq), lambda hd, i: (hd, i)),
            pl.BlockSpec((s, HEAD_DIM), lambda hd, i: (0, hd // group)),
            pl.BlockSpec((None, nk, HEAD_DIM, tk), lambda hd, i: (hd // group, 0, 0, 0)),
        ],
        out_specs=pl.BlockSpec((tq, HEAD_DIM), lambda hd, i: (i, hd)),
        out_shape=jax.ShapeDtypeStruct((s, d_gqa), BF16),
        scratch_shapes=[pltpu.VMEM((tk, tq), F32), pltpu.VMEM((tk, tq), F32)],
        compiler_params=pltpu.CompilerParams(
            dimension_semantics=("parallel", "arbitrary"), vmem_limit_bytes=VMEM_LIMIT),
        name="gqa_attn",
    )(qt, kg, vt)


def _out_proj_ln_kernel(ona_ref, og_ref, x_ref, w_ref, gna_ref, gg_ref, g_ref, b_ref, o_ref, ob_ref, *, alpha):
    for r0 in range(0, ona_ref.shape[0], PROJ_SUB):
        rows = slice(r0, r0 + PROJ_SUB)
        a = ona_ref[rows, :].astype(F32)
        b = og_ref[rows, :].astype(F32)
        mixed_in = jnp.concatenate([(a * _rms_scale(a) * gna_ref[...]).astype(BF16),
                                    (b * _rms_scale(b) * gg_ref[...]).astype(BF16)], axis=1)
        mix = jnp.dot(mixed_in, w_ref[...], preferred_element_type=F32)
        out = _layer_norm(alpha * x_ref[rows, :] + mix, g_ref[...], b_ref[...])
        o_ref[rows, :] = out
        ob_ref[rows, :] = out.astype(ob_ref.dtype)


def _out_proj_ln(o_na, o_g, x, w_out, gn_na, gn_g, ln_g, ln_b, layer, alpha):
    s, d = x.shape
    d_na = o_na.shape[1]
    d_g = o_g.shape[1]
    tm = PROJ_TM
    return pl.pallas_call(
        functools.partial(_out_proj_ln_kernel, alpha=alpha),
        grid=(s // tm,),
        in_specs=[
            pl.BlockSpec((tm, d_na), lambda i: (i, 0)),
            pl.BlockSpec((tm, d_g), lambda i: (i, 0)),
            pl.BlockSpec((tm, d), lambda i: (i, 0)),
            pl.BlockSpec((d_na + d_g, d), lambda i: (0, 0)),
            pl.BlockSpec((None, 1, d_na), lambda i: (layer, 0, 0)),
            pl.BlockSpec((None, 1, d_g), lambda i: (layer, 0, 0)),
            pl.BlockSpec((None, 1, d), lambda i: (layer, 0, 0)),
            pl.BlockSpec((None, 1, d), lambda i: (layer, 0, 0)),
        ],
        out_specs=[pl.BlockSpec((tm, d), lambda i: (i, 0)), pl.BlockSpec((tm, d), lambda i: (i, 0))],
        out_shape=[jax.ShapeDtypeStruct((s, d), F32), jax.ShapeDtypeStruct((s, d), BF16)],
        compiler_params=pltpu.CompilerParams(
            dimension_semantics=("parallel",), vmem_limit_bytes=VMEM_LIMIT),
        name="out_proj_ln",
    )(o_na, o_g, x, w_out, gn_na, gn_g, ln_g, ln_b)


def _rope_tables(s):
    half = HEAD_DIM // 2
    nfreq = half // 2
    t = np.arange(s)
    row = (t // GRID_W).astype(np.float32)
    col = (t % GRID_W).astype(np.float32)
    inv_freq = (1.0 / (np.float32(ROPE_THETA) ** (np.arange(nfreq, dtype=np.float32) / np.float32(nfreq)))
                ).astype(np.float32)
    lane = np.arange(HEAD_DIM)
    pos = np.where(lane[None, :] < half, row[:, None], col[:, None])
    ang = (pos * inv_freq[lane % nfreq][None, :]).astype(np.float32)
    cos, sin = np.cos(ang), np.sin(ang)
    lower = (lane % half) < nfreq
    sin_lo = np.where(lower[None, :], -sin, np.float32(0.0))
    sin_hi = np.where(lower[None, :], np.float32(0.0), sin)
    return tuple(jnp.asarray(a, F32) for a in (cos, sin_lo, sin_hi))


def kernel(x, ffn1_w_gate_up, ffn1_w_down, ln1_g, ln1_b, w_in, na_rel_bias, q_norm_g, k_norm_g,
           gn_na_g, gn_gqa_g, w_out, ln2_g, ln2_b, ffn2_w_gate_up, ffn2_w_down, ln3_g, ln3_b):
    batch, s, d = x.shape
    depth = w_in.shape[0]
    alpha = (2.0 * depth) ** 0.25

    def row(p):
        return p.astype(F32)[:, None, :]

    cos, sin_lo, sin_hi = _rope_tables(s)
    bias = _na_bias_tiles(na_rel_bias)
    bias_idx = _na_bias_index(s // GRID_W)
    ln1g, ln1b, ln2g, ln2b, ln3g, ln3b = map(row, (ln1_g, ln1_b, ln2_g, ln2_b, ln3_g, ln3_b))
    qg, kg, gna, ggq = map(row, (q_norm_g, k_norm_g, gn_na_g, gn_gqa_g))

    outs = []
    for bi in range(batch):
        xs = x[bi]
        xb = xs.astype(BF16)
        for layer in range(depth):
            hid, wd_b = _ffn_up(xb, ffn1_w_gate_up, layer, cast=(ffn1_w_down,))
            xs, xb, w_in_b, w_out_b = _ffn_down_ln(hid, wd_b, xs, ln1g, ln1b, layer, alpha, cast=(w_in, w_out))
            qn_t, k_n, vn_t, q_t, k_g, v_t = _in_proj(xb, w_in_b, cos, sin_lo, sin_hi, qg, kg, layer)
            o_na = _na_attn(qn_t, k_n, vn_t, bias, bias_idx, layer)
            o_g = _gqa_attn(q_t, k_g, v_t)
            xs, xb = _out_proj_ln(o_na, o_g, xs, w_out_b, gna, ggq, ln2g, ln2b, layer, alpha)
            hid, wd_b = _ffn_up(xb, ffn2_w_gate_up, layer, cast=(ffn2_w_down,))
            xs, xb = _ffn_down_ln(hid, wd_b, xs, ln3g, ln3b, layer, alpha)
        outs.append(xs)
    return outs[0][None] if batch == 1 else jnp.stack(outs, axis=0)
```

```python
import functools

import jax
import jax.numpy as jnp
import numpy as np
from jax import lax
from jax.experimental import pallas as pl
from jax.experimental.pallas import tpu as pltpu

F32 = jnp.float32
BF16 = jnp.bfloat16

GRID_W = 64
HEAD_DIM = 128
N_KV_HEADS = 2
NA_KH = 8
NA_KW = 16
ROPE_THETA = 10000.0
RMS_EPS = 1e-6
LN_EPS = 1e-5
LOG2E = 1.4426950408889634
SM_SCALE = HEAD_DIM ** -0.5 * LOG2E
MASK_VALUE = -1e30

FFN_UP_TM = 2048
FFN_TF = 512
FFN_SUB = 512
FFN_COLS = 256
FFN_DOWN_TM = 256
FFN_DOWN_SUB = 128
PROJ_TM = 512
PROJ_TN = 256
PROJ_SUB = 256
GQA_TQ = 2048
GQA_TK = 512
GQA_QB = 256
GQA_UNROLL = 4
NA_G = 4
NA_SLAB = 12
NA_BLOCKS = 16
NA_TOK = 256
NA_AHEAD = 3
VMEM_LIMIT = 56 * 1024 * 1024


def _layer_norm(y, g, b):
    mu = jnp.mean(y, axis=-1, keepdims=True)
    yc = y - mu
    var = jnp.mean(yc * yc, axis=-1, keepdims=True)
    return yc * lax.rsqrt(var + LN_EPS) * g + b


def _rms_scale(t):
    return lax.rsqrt(jnp.mean(t * t, axis=-1, keepdims=True) + RMS_EPS)


def _fold_row_groups(x, op):
    out = x[0:8]
    for r in range(8, x.shape[0], 8):
        out = op(out, x[r:r + 8])
    return out


def _cast_specs(weights, layer, n_steps, step_of):
    in_specs, out_specs, out_shapes = [], [], []
    for w in weights:
        _, r, c = w.shape
        rb = r // n_steps
        assert rb * n_steps == r and rb % 16 == 0
        in_specs.append(pl.BlockSpec((None, rb, c), lambda *g: (layer, step_of(*g), 0)))
        out_specs.append(pl.BlockSpec((rb, c), lambda *g: (step_of(*g), 0)))
        out_shapes.append(jax.ShapeDtypeStruct((r, c), BF16))
    return in_specs, out_specs, out_shapes


def _ffn_up_kernel(xb_ref, wg_ref, wu_ref, *rest):
    n_cast = len(rest) // 2
    h_ref = rest[n_cast]
    for src, dst in zip(rest[:n_cast], rest[n_cast + 1:]):
        dst[...] = src[...].astype(dst.dtype)
    tm, tf = h_ref.shape
    col_slices = [slice(c0, c0 + FFN_COLS) for c0 in range(0, tf, FFN_COLS)]
    wg = [wg_ref[:, cs].astype(BF16) for cs in col_slices]
    wu = [wu_ref[:, cs].astype(BF16) for cs in col_slices]
    bounds = list(range(0, tm - FFN_SUB, FFN_SUB)) + [tm - FFN_SUB, tm - FFN_SUB // 2, tm]
    for r0, r1 in zip(bounds[:-1], bounds[1:]):
        xb = xb_ref[r0:r1, :]
        for cs, wgc, wuc in zip(col_slices, wg, wu):
            gate = jnp.dot(xb, wgc, preferred_element_type=F32)
            up = jnp.dot(xb, wuc, preferred_element_type=F32)
            h_ref[r0:r1, cs] = (gate / (1.0 + jnp.exp(-gate)) * up).astype(h_ref.dtype)


def _ffn_up(xb, w_gu, layer, cast=()):
    s, d = xb.shape
    d_ff = w_gu.shape[2] // 2
    tm, tf = FFN_UP_TM, FFN_TF
    assert s % tm == 0 and d_ff % tf == 0 and tm % FFN_SUB == 0
    nf = d_ff // tf
    c_in, c_out, c_shapes = _cast_specs(cast, layer, (s // tm) * nf, lambda i, j: i * nf + j)
    return pl.pallas_call(
        _ffn_up_kernel,
        grid=(s // tm, nf),
        in_specs=[
            pl.BlockSpec((tm, d), lambda i, j: (i, 0)),
            pl.BlockSpec((None, d, tf), lambda i, j: (layer, 0, j)),
            pl.BlockSpec((None, d, tf), lambda i, j: (layer, 0, j + nf)),
        ] + c_in,
        out_specs=[pl.BlockSpec((tm, tf), lambda i, j: (i, j))] + c_out,
        out_shape=[jax.ShapeDtypeStruct((s, d_ff), BF16)] + c_shapes,
        compiler_params=pltpu.CompilerParams(
            dimension_semantics=("arbitrary", "arbitrary"), vmem_limit_bytes=VMEM_LIMIT),
        name="ffn_up",
    )(xb, w_gu, w_gu, *cast)


def _ffn_down_ln_kernel(h_ref, wd_ref, x_ref, g_ref, b_ref, *rest, alpha):
    n_cast = (len(rest) - 2) // 2
    o_ref, ob_ref = rest[n_cast], rest[n_cast + 1]
    for src, dst in zip(rest[:n_cast], rest[n_cast + 2:]):
        dst[...] = src[...].astype(dst.dtype)
    for r0 in range(0, x_ref.shape[0], FFN_DOWN_SUB):
        rows = slice(r0, r0 + FFN_DOWN_SUB)
        y = alpha * x_ref[rows, :] + 0.5 * jnp.dot(h_ref[rows, :], wd_ref[...], preferred_element_type=F32)
        out = _layer_norm(y, g_ref[...], b_ref[...])
        o_ref[rows, :] = out
        ob_ref[rows, :] = out.astype(ob_ref.dtype)


def _ffn_down_ln(h, w_d, x, ln_g, ln_b, layer, alpha, cast=()):
    s, d = x.shape
    d_ff = h.shape[1]
    tm = FFN_DOWN_TM
    assert s % tm == 0 and tm % FFN_DOWN_SUB == 0
    c_in, c_out, c_shapes = _cast_specs(cast, layer, s // tm, lambda i: i)
    return pl.pallas_call(
        functools.partial(_ffn_down_ln_kernel, alpha=alpha),
        grid=(s // tm,),
        in_specs=[
            pl.BlockSpec((tm, d_ff), lambda i: (i, 0)),
            pl.BlockSpec((d_ff, d), lambda i: (0, 0), pipeline_mode=pl.Buffered(1)),
            pl.BlockSpec((tm, d), lambda i: (i, 0)),
            pl.BlockSpec((None, 1, d), lambda i: (layer, 0, 0)),
            pl.BlockSpec((None, 1, d), lambda i: (layer, 0, 0)),
        ] + c_in,
        out_specs=[pl.BlockSpec((tm, d), lambda i: (i, 0)), pl.BlockSpec((tm, d), lambda i: (i, 0))] + c_out,
        out_shape=[jax.ShapeDtypeStruct((s, d), F32), jax.ShapeDtypeStruct((s, d), BF16)] + c_shapes,
        compiler_params=pltpu.CompilerParams(
            dimension_semantics=("arbitrary",), vmem_limit_bytes=VMEM_LIMIT),
        name="ffn_down_ln",
    )(h, w_d, x, ln_g, ln_b, *cast)


def _rope(t, cos, sin_lo, sin_hi):
    return t * cos + pltpu.roll(t, 96, 1) * sin_lo + pltpu.roll(t, 32, 1) * sin_hi


def _in_proj_kernel(xb_ref, w_ref, cos_ref, slo_ref, shi_ref, qg_ref, kg_ref,
                    qnt_ref, kna_ref, vnt_ref, qt_ref, kgo_ref, vt_ref, *, d_na):
    tn = PROJ_TN
    heads = tn // HEAD_DIM
    tm = xb_ref.shape[0]

    def cols(c0):
        return jnp.dot(xb_ref[...], w_ref[:, c0:c0 + tn], preferred_element_type=F32)

    def head(acc, h):
        return acc[:, h * HEAD_DIM:(h + 1) * HEAD_DIM]

    def normed_rope(t, gain):
        t = t * _rms_scale(t) * gain
        return _rope(t, cos_ref[...], slo_ref[...], shi_ref[...])

    for c in range(0, d_na, tn):
        acc = cols(3 * d_na + c)
        for h in range(heads):
            t = normed_rope(head(acc, h), qg_ref[...]) * SM_SCALE
            qt_ref[c + h * HEAD_DIM:c + (h + 1) * HEAD_DIM, :] = t.T.astype(qt_ref.dtype)
    acc = cols(4 * d_na)
    for h in range(N_KV_HEADS):
        kgo_ref[:, h * HEAD_DIM:(h + 1) * HEAD_DIM] = normed_rope(head(acc, h), kg_ref[...]).astype(kgo_ref.dtype)
    acc = cols(4 * d_na + N_KV_HEADS * HEAD_DIM)
    for h in range(N_KV_HEADS):
        vt_ref[h, 0] = head(acc, h).T.astype(vt_ref.dtype)
    for c in range(0, d_na, tn):
        acc = cols(c) * SM_SCALE
        for h in range(heads):
            qnt_ref[c + h * HEAD_DIM:c + (h + 1) * HEAD_DIM, :] = head(acc, h).T.astype(qnt_ref.dtype)
    for c in range(0, d_na, tn):
        acc = cols(2 * d_na + c)
        for h in range(heads):
            vh = head(acc, h).T.astype(vnt_ref.dtype)
            for t0 in range(0, tm, NA_TOK):
                vnt_ref[c // HEAD_DIM + h, t0 // NA_TOK] = vh[:, t0:t0 + NA_TOK]
    for c in range(0, d_na, tn):
        kna_ref[:, c:c + tn] = cols(d_na + c).astype(kna_ref.dtype)


def _in_proj(xb, w_in, cos, sin_lo, sin_hi, q_gain, k_gain, layer):
    s, d = xb.shape
    d_in = w_in.shape[1]
    tm, tn = PROJ_TM, PROJ_TN
    d_na = d // 2
    n_heads_na = d_na // HEAD_DIM
    d_kv = N_KV_HEADS * HEAD_DIM
    assert d_na % tn == 0 and d_kv == tn and d_in == 4 * d_na + 2 * d_kv and tm == GQA_TK and tm % NA_TOK == 0
    return pl.pallas_call(
        functools.partial(_in_proj_kernel, d_na=d_na),
        grid=(s // tm,),
        in_specs=[
            pl.BlockSpec((tm, d), lambda i: (i, 0)),
            pl.BlockSpec((d, d_in), lambda i: (0, 0), pipeline_mode=pl.Buffered(1)),
            pl.BlockSpec((tm, HEAD_DIM), lambda i: (i, 0)),
            pl.BlockSpec((tm, HEAD_DIM), lambda i: (i, 0)),
            pl.BlockSpec((tm, HEAD_DIM), lambda i: (i, 0)),
            pl.BlockSpec((None, 1, HEAD_DIM), lambda i: (layer, 0, 0)),
            pl.BlockSpec((None, 1, HEAD_DIM), lambda i: (layer, 0, 0)),
        ],
        out_specs=[
            pl.BlockSpec((d_na, tm), lambda i: (0, i)),
            pl.BlockSpec((tm, d_na), lambda i: (i, 0)),
            pl.BlockSpec((n_heads_na, tm // NA_TOK, HEAD_DIM, NA_TOK), lambda i: (0, i, 0, 0)),
            pl.BlockSpec((d_na, tm), lambda i: (0, i)),
            pl.BlockSpec((tm, d_kv), lambda i: (i, 0)),
            pl.BlockSpec((N_KV_HEADS, 1, HEAD_DIM, tm), lambda i: (0, i, 0, 0)),
        ],
        out_shape=[
            jax.ShapeDtypeStruct((d_na, s), BF16),
            jax.ShapeDtypeStruct((s, d_na), BF16),
            jax.ShapeDtypeStruct((n_heads_na, s // NA_TOK, HEAD_DIM, NA_TOK), BF16),
            jax.ShapeDtypeStruct((d_na, s), BF16),
            jax.ShapeDtypeStruct((s, d_kv), BF16),
            jax.ShapeDtypeStruct((N_KV_HEADS, s // tm, HEAD_DIM, tm), BF16),
        ],
        compiler_params=pltpu.CompilerParams(
            dimension_semantics=("parallel",), vmem_limit_bytes=VMEM_LIMIT),
        name="in_proj",
    )(xb, w_in, cos, sin_lo, sin_hi, q_gain, k_gain)


def _na_kernel(idx_ref, qt_ref, k_ref, vt_ref, bias_ref, o_ref, *, blocks_per_step, n_rows):
    step = pl.program_id(1)
    n_blocks = n_rows // NA_G
    tq = NA_G * GRID_W
    slab = NA_SLAB * GRID_W
    pairs = NA_G // 2

    def scores(i):
        b = step * blocks_per_step + i
        start_row = jnp.clip(b * NA_G - NA_KH // 2, 0, n_rows - NA_SLAB)
        start = pl.multiple_of(start_row * GRID_W, NA_TOK)
        k = k_ref[pl.ds(start, slab), :]
        st = jnp.dot(k, qt_ref[:, i * tq:(i + 1) * tq], preferred_element_type=F32)
        pattern = jnp.where(b == 0, 0, jnp.where(b == n_blocks - 1, 2, 1))
        rows = []
        for u in range(NA_SLAB):
            tiles = [st[u * GRID_W:(u + 1) * GRID_W, j * 2 * GRID_W:(j + 1) * 2 * GRID_W]
                     + bias_ref[idx_ref[(pattern * NA_SLAB + u) * pairs + j]] for j in range(pairs)]
            rows.append(jnp.concatenate(tiles, axis=1))
        return jnp.concatenate(rows, axis=0), start_row // (NA_TOK // GRID_W)

    ahead = [scores(i) for i in range(min(NA_AHEAD, blocks_per_step))]
    for i in range(blocks_per_step):
        st, tok_blk = ahead.pop(0)
        if i + NA_AHEAD < blocks_per_step:
            ahead.append(scores(i + NA_AHEAD))
        m = jnp.max(_fold_row_groups(st, jnp.maximum), axis=0, keepdims=True)
        p = jnp.exp2(st - m)
        l = jnp.sum(_fold_row_groups(p, jnp.add), axis=0, keepdims=True)
        vt = jnp.concatenate([vt_ref[tok_blk + t] for t in range(slab // NA_TOK)], axis=1)
        ot = jnp.dot(vt, p.astype(BF16), preferred_element_type=F32) / l
        o_ref[i * tq:(i + 1) * tq, :] = ot.T.astype(o_ref.dtype)


def _na_attn(qnt, kna, vnt, bias, bias_idx, layer):
    d_na, s = qnt.shape
    n_heads = d_na // HEAD_DIM
    n_rows = s // GRID_W
    blocks_per_step = NA_BLOCKS
    rows_per_step = blocks_per_step * NA_G
    tok_rows = NA_TOK // GRID_W
    assert n_rows % rows_per_step == 0 and n_rows >= NA_SLAB and NA_G % 2 == 0
    assert NA_G % tok_rows == 0 and (NA_KH // 2) % tok_rows == 0 and NA_SLAB % tok_rows == 0
    tq = rows_per_step * GRID_W
    n_tiles = bias.shape[2]
    return pl.pallas_call(
        functools.partial(_na_kernel, blocks_per_step=blocks_per_step, n_rows=n_rows),
        grid=(n_heads, n_rows // rows_per_step),
        in_specs=[
            pl.BlockSpec(memory_space=pltpu.SMEM),
            pl.BlockSpec((HEAD_DIM, tq), lambda hd, rb: (hd, rb)),
            pl.BlockSpec((s, HEAD_DIM), lambda hd, rb: (0, hd)),
            pl.BlockSpec((None, s // NA_TOK, HEAD_DIM, NA_TOK), lambda hd, rb: (hd, 0, 0, 0)),
            pl.BlockSpec((None, None, n_tiles, GRID_W, 2 * GRID_W), lambda hd, rb: (layer, hd, 0, 0, 0)),
        ],
        out_specs=pl.BlockSpec((tq, HEAD_DIM), lambda hd, rb: (rb, hd)),
        out_shape=jax.ShapeDtypeStruct((s, d_na), BF16),
        compiler_params=pltpu.CompilerParams(
            dimension_semantics=("parallel", "arbitrary"), vmem_limit_bytes=VMEM_LIMIT),
        name="na_attn",
    )(bias_idx, qnt, kna, vnt, bias)


def _na_bias_tiles(rel_bias):
    w, kw = GRID_W, NA_KW
    nl, nh, nd, ne = rel_bias.shape
    c = np.arange(w)
    dcol = np.clip(c[:, None] - c[None, :], -(kw - 1), kw - 1) + (kw - 1)
    onehot = (dcol.reshape(-1)[None, :] == np.arange(ne)[:, None]).astype(np.float32)
    toep = jnp.dot(rel_bias.astype(F32).reshape(nl * nh * nd, ne) * LOG2E, jnp.asarray(onehot),
                   precision=lax.Precision.HIGHEST).reshape(nl, nh, nd, w, w)
    c0 = np.clip(c - kw // 2, 0, w - kw)
    col_in = (c[:, None] >= c0[None, :]) & (c[:, None] < c0[None, :] + kw)
    toep = jnp.where(jnp.asarray(col_in), toep, MASK_VALUE)
    masked = jnp.full((nl, nh, 1, w, w), MASK_VALUE, F32)
    ext = jnp.concatenate([masked, toep, masked], axis=2)
    first, second = ext[:, :, 1:], ext[:, :, :-1]
    blank = jnp.broadcast_to(masked, first.shape)
    return jnp.concatenate([jnp.concatenate([first, second], axis=-1),
                            jnp.concatenate([blank, second], axis=-1),
                            jnp.concatenate([first, blank], axis=-1)], axis=2)


def _na_bias_index(n_rows):
    n_var = 2 * NA_KH
    first, interior, last = (0, 0), (1, -(NA_KH // 2)), (n_rows // NA_G - 1, NA_G - NA_SLAB)
    idx = []
    for b, rel in (first, interior, last):
        lo = [min(max(b * NA_G + rr - NA_KH // 2, 0), n_rows - NA_KH) - (b * NA_G + rel) for rr in range(NA_G)]
        for u in range(NA_SLAB):
            for rr in range(0, NA_G, 2):
                has = [lo[r] <= u < lo[r] + NA_KH for r in (rr, rr + 1)]
                f = u + rel - rr + NA_KH - 1
                if has[0] and has[1]:
                    idx.append(f)
                elif has[1]:
                    idx.append(n_var + f)
                elif has[0]:
                    idx.append(2 * n_var + f)
                else:
                    idx.append(3 * n_var - 1)
    return jnp.asarray(idx, jnp.int32)


def _gqa_kernel(qt_ref, k_ref, vt_ref, o_ref, sa_ref, sb_ref, *, tk, nk):
    tq = qt_ref.shape[1]
    qt = qt_ref[...]

    qblocks = [slice(n, n + GQA_QB) for n in range(0, tq, GQA_QB)]

    def key_chunk(c):
        return k_ref[pl.ds(pl.multiple_of(c * tk, tk), tk), :]

    def scores_block(k, qb, dst_ref):
        st = jnp.dot(k, qt[:, qb], preferred_element_type=F32)
        dst_ref[:, qb] = st
        return _fold_row_groups(st, jnp.maximum)

    def scores(c, dst_ref):
        k = key_chunk(c)
        return jnp.concatenate([scores_block(k, qb, dst_ref) for qb in qblocks], axis=1)

    def step(c, src_ref, dst_ref, m8, m, l8, acc):
        m_new = jnp.maximum(m, jnp.max(m8, axis=0, keepdims=True))
        corr = jnp.exp2(m - m_new)
        vt = vt_ref[c]
        k_next = None if dst_ref is None else key_chunk(c + 1)
        m8_next, l8_new, acc_new = [], [], []
        for qb in qblocks:
            if dst_ref is not None:
                m8_next.append(scores_block(k_next, qb, dst_ref))
            p = jnp.exp2(src_ref[:, qb] - m_new[:, qb])
            l8_new.append(corr[:, qb] * l8[:, qb] + _fold_row_groups(p, jnp.add))
            acc_new.append(corr[:, qb] * acc[:, qb]
                           + jnp.dot(vt, p.astype(BF16), preferred_element_type=F32))
        m8 = jnp.concatenate(m8_next, axis=1) if m8_next else m8
        return m8, m_new, jnp.concatenate(l8_new, axis=1), jnp.concatenate(acc_new, axis=1)

    def group(c0, m8, m, l8, acc, last):
        bufs = (sa_ref, sb_ref)
        for t in range(GQA_UNROLL):
            more = t + 1 < GQA_UNROLL or not last
            m8, m, l8, acc = step(c0 + t, bufs[t % 2], bufs[(t + 1) % 2] if more else None, m8, m, l8, acc)
        return m8, m, l8, acc

    init = (scores(0, sa_ref), jnp.full((1, tq), -jnp.inf, F32), jnp.zeros((8, tq), F32),
            jnp.zeros((HEAD_DIM, tq), F32))
    carry = lax.fori_loop(0, nk // GQA_UNROLL - 1,
                          lambda j, cr: group(GQA_UNROLL * j, *cr, last=False), init)
    _, _, l8, acc = group(nk - GQA_UNROLL, *carry, last=True)
    ot = acc / jnp.sum(l8, axis=0, keepdims=True)
    o_ref[...] = ot.T.astype(o_ref.dtype)


def _gqa_attn(qt, kg, vt):
    d_gqa, s = qt.shape
    n_heads = d_gqa // HEAD_DIM
    group = n_heads // N_KV_HEADS
    tq, tk = GQA_TQ, GQA_TK
    assert s % tq == 0 and s % (GQA_UNROLL * tk) == 0 and GQA_UNROLL % 2 == 0
    nk = s // tk
    return pl.pallas_call(
        functools.partial(_gqa_kernel, tk=tk, nk=nk),
        grid=(n_heads, s // tq),
        in_specs=[
            pl.BlockSpec((HEAD_DIM, tq), lambda hd, i: (hd, i)),
            pl.BlockSpec((s, HEAD_DIM), lambda hd, i: (0, hd // group)),
            pl.BlockSpec((None, nk, HEAD_DIM, tk), lambda hd, i: (hd // group, 0, 0, 0)),
        ],
        out_specs=pl.BlockSpec((tq, HEAD_DIM), lambda hd, i: (i, hd)),
        out_shape=jax.ShapeDtypeStruct((s, d_gqa), BF16),
        scratch_shapes=[pltpu.VMEM((tk, tq), F32), pltpu.VMEM((tk, tq), F32)],
        compiler_params=pltpu.CompilerParams(
            dimension_semantics=("parallel", "arbitrary"), vmem_limit_bytes=VMEM_LIMIT),
        name="gqa_attn",
    )(qt, kg, vt)


def _out_proj_ln_kernel(ona_ref, og_ref, x_ref, w_ref, gna_ref, gg_ref, g_ref, b_ref, o_ref, ob_ref, *, alpha):
    for r0 in range(0, ona_ref.shape[0], PROJ_SUB):
        rows = slice(r0, r0 + PROJ_SUB)
        a = ona_ref[rows, :].astype(F32)
        b = og_ref[rows, :].astype(F32)
        mixed_in = jnp.concatenate([(a * _rms_scale(a) * gna_ref[...]).astype(BF16),
                                    (b * _rms_scale(b) * gg_ref[...]).astype(BF16)], axis=1)
        mix = jnp.dot(mixed_in, w_ref[...], preferred_element_type=F32)
        out = _layer_norm(alpha * x_ref[rows, :] + mix, g_ref[...], b_ref[...])
        o_ref[rows, :] = out
        ob_ref[rows, :] = out.astype(ob_ref.dtype)


def _out_proj_ln(o_na, o_g, x, w_out, gn_na, gn_g, ln_g, ln_b, layer, alpha):
    s, d = x.shape
    d_na = o_na.shape[1]
    d_g = o_g.shape[1]
    tm = PROJ_TM
    return pl.pallas_call(
        functools.partial(_out_proj_ln_kernel, alpha=alpha),
        grid=(s // tm,),
        in_specs=[
            pl.BlockSpec((tm, d_na), lambda i: (i, 0)),
            pl.BlockSpec((tm, d_g), lambda i: (i, 0)),
            pl.BlockSpec((tm, d), lambda i: (i, 0)),
            pl.BlockSpec((d_na + d_g, d), lambda i: (0, 0)),
            pl.BlockSpec((None, 1, d_na), lambda i: (layer, 0, 0)),
            pl.BlockSpec((None, 1, d_g), lambda i: (layer, 0, 0)),
            pl.BlockSpec((None, 1, d), lambda i: (layer, 0, 0)),
            pl.BlockSpec((None, 1, d), lambda i: (layer, 0, 0)),
        ],
        out_specs=[pl.BlockSpec((tm, d), lambda i: (i, 0)), pl.BlockSpec((tm, d), lambda i: (i, 0))],
        out_shape=[jax.ShapeDtypeStruct((s, d), F32), jax.ShapeDtypeStruct((s, d), BF16)],
        compiler_params=pltpu.CompilerParams(
            dimension_semantics=("parallel",), vmem_limit_bytes=VMEM_LIMIT),
        name="out_proj_ln",
    )(o_na, o_g, x, w_out, gn_na, gn_g, ln_g, ln_b)


def _rope_tables(s):
    half = HEAD_DIM // 2
    nfreq = half // 2
    t = np.arange(s)
    row = (t // GRID_W).astype(np.float32)
    col = (t % GRID_W).astype(np.float32)
    inv_freq = (1.0 / (np.float32(ROPE_THETA) ** (np.arange(nfreq, dtype=np.float32) / np.float32(nfreq)))
                ).astype(np.float32)
    lane = np.arange(HEAD_DIM)
    pos = np.where(lane[None, :] < half, row[:, None], col[:, None])
    ang = (pos * inv_freq[lane % nfreq][None, :]).astype(np.float32)
    cos, sin = np.cos(ang), np.sin(ang)
    lower = (lane % half) < nfreq
    sin_lo = np.where(lower[None, :], -sin, np.float32(0.0))
    sin_hi = np.where(lower[None, :], np.float32(0.0), sin)
    return tuple(jnp.asarray(a, F32) for a in (cos, sin_lo, sin_hi))


def kernel(x, ffn1_w_gate_up, ffn1_w_down, ln1_g, ln1_b, w_in, na_rel_bias, q_norm_g, k_norm_g,
           gn_na_g, gn_gqa_g, w_out, ln2_g, ln2_b, ffn2_w_gate_up, ffn2_w_down, ln3_g, ln3_b):
    batch, s, d = x.shape
    depth = w_in.shape[0]
    alpha = (2.0 * depth) ** 0.25

    def row(p):
        return p.astype(F32).reshape(depth, 1, p.shape[-1])

    cos, sin_lo, sin_hi = _rope_tables(s)
    bias = _na_bias_tiles(na_rel_bias)
    bias_idx = _na_bias_index(s // GRID_W)
    ln1g, ln1b, ln2g, ln2b, ln3g, ln3b = map(row, (ln1_g, ln1_b, ln2_g, ln2_b, ln3_g, ln3_b))
    qg, kg, gna, ggq = map(row, (q_norm_g, k_norm_g, gn_na_g, gn_gqa_g))

    outs = []
    for bi in range(batch):
        xs = x[bi]
        xb = xs.astype(BF16)
        for layer in range(depth):
            hid, wd_b = _ffn_up(xb, ffn1_w_gate_up, layer, cast=(ffn1_w_down,))
            xs, xb, w_in_b, w_out_b = _ffn_down_ln(hid, wd_b, xs, ln1g, ln1b, layer, alpha, cast=(w_in, w_out))
            qn_t, k_n, vn_t, q_t, k_g, v_t = _in_proj(xb, w_in_b, cos, sin_lo, sin_hi, qg, kg, layer)
            o_na = _na_attn(qn_t, k_n, vn_t, bias, bias_idx, layer)
            o_g = _gqa_attn(q_t, k_g, v_t)
            xs, xb = _out_proj_ln(o_na, o_g, xs, w_out_b, gna, ggq, ln2g, ln2b, layer, alpha)
            hid, wd_b = _ffn_up(xb, ffn2_w_gate_up, layer, cast=(ffn2_w_down,))
            xs, xb = _ffn_down_ln(hid, wd_b, xs, ln3g, ln3b, layer, alpha)
        outs.append(xs)
    return outs[0][None] if batch == 1 else jnp.stack(outs, axis=0)
```

```python
import functools

import jax
import jax.numpy as jnp
import numpy as np
from jax import lax
from jax.experimental import pallas as pl
from jax.experimental.pallas import tpu as pltpu

F32 = jnp.float32
BF16 = jnp.bfloat16

GRID_W = 64
HEAD_DIM = 128
N_KV_HEADS = 2
NA_KH = 8
NA_KW = 16
ROPE_THETA = 10000.0
RMS_EPS = 1e-6
LN_EPS = 1e-5
LOG2E = 1.4426950408889634
SM_SCALE = HEAD_DIM ** -0.5 * LOG2E
MASK_VALUE = -1e30

FFN_UP_TM = 2048
FFN_TF = 512
FFN_SUB = 512
FFN_COLS = 256
FFN_DOWN_TM = 256
FFN_DOWN_SUB = 128
PROJ_TM = 512
PROJ_TN = 256
PROJ_SUB = 256
GQA_TQ = 2048
GQA_TK = 512
GQA_QB = 256
GQA_UNROLL = 4
NA_G = 4
NA_SLAB = 12
NA_BLOCKS = 16
NA_TOK = 256
NA_AHEAD = 4
VMEM_LIMIT = 56 * 1024 * 1024


def _layer_norm(y, g, b):
    mu = jnp.mean(y, axis=-1, keepdims=True)
    yc = y - mu
    var = jnp.mean(yc * yc, axis=-1, keepdims=True)
    return yc * lax.rsqrt(var + LN_EPS) * g + b


def _rms_scale(t):
    return lax.rsqrt(jnp.mean(t * t, axis=-1, keepdims=True) + RMS_EPS)


def _param_spec(p):
    return pl.BlockSpec(p.shape, lambda *g: (0, 0))


def _param_row(ref, layer):
    return ref[layer:layer + 1, :]


def _fold_row_groups(x, op):
    out = x[0:8]
    for r in range(8, x.shape[0], 8):
        out = op(out, x[r:r + 8])
    return out


def _cast_specs(weights, layer, n_steps, step_of):
    in_specs, out_specs, out_shapes = [], [], []
    for w in weights:
        _, r, c = w.shape
        rb = r // n_steps
        assert rb * n_steps == r and rb % 16 == 0
        in_specs.append(pl.BlockSpec((None, rb, c), lambda *g: (layer, step_of(*g), 0)))
        out_specs.append(pl.BlockSpec((rb, c), lambda *g: (step_of(*g), 0)))
        out_shapes.append(jax.ShapeDtypeStruct((r, c), BF16))
    return in_specs, out_specs, out_shapes


def _ffn_up_kernel(xb_ref, wg_ref, wu_ref, *rest):
    n_cast = len(rest) // 2
    h_ref = rest[n_cast]
    for src, dst in zip(rest[:n_cast], rest[n_cast + 1:]):
        dst[...] = src[...].astype(dst.dtype)
    tm, tf = h_ref.shape
    col_slices = [slice(c0, c0 + FFN_COLS) for c0 in range(0, tf, FFN_COLS)]
    wg = [wg_ref[:, cs].astype(BF16) for cs in col_slices]
    wu = [wu_ref[:, cs].astype(BF16) for cs in col_slices]
    bounds = list(range(0, tm - FFN_SUB, FFN_SUB)) + [tm - FFN_SUB, tm - FFN_SUB // 2, tm]
    for r0, r1 in zip(bounds[:-1], bounds[1:]):
        xb = xb_ref[r0:r1, :]
        for cs, wgc, wuc in zip(col_slices, wg, wu):
            gate = jnp.dot(xb, wgc, preferred_element_type=F32)
            up = jnp.dot(xb, wuc, preferred_element_type=F32)
            h_ref[r0:r1, cs] = (gate / (1.0 + jnp.exp(-gate)) * up).astype(h_ref.dtype)


def _ffn_up(xb, w_gu, layer, cast=()):
    s, d = xb.shape
    d_ff = w_gu.shape[2] // 2
    tm, tf = FFN_UP_TM, FFN_TF
    assert s % tm == 0 and d_ff % tf == 0 and tm % FFN_SUB == 0
    nf = d_ff // tf
    c_in, c_out, c_shapes = _cast_specs(cast, layer, (s // tm) * nf, lambda i, j: i * nf + j)
    return pl.pallas_call(
        _ffn_up_kernel,
        grid=(s // tm, nf),
        in_specs=[
            pl.BlockSpec((tm, d), lambda i, j: (i, 0)),
            pl.BlockSpec((None, d, tf), lambda i, j: (layer, 0, j)),
            pl.BlockSpec((None, d, tf), lambda i, j: (layer, 0, j + nf)),
        ] + c_in,
        out_specs=[pl.BlockSpec((tm, tf), lambda i, j: (i, j))] + c_out,
        out_shape=[jax.ShapeDtypeStruct((s, d_ff), BF16)] + c_shapes,
        compiler_params=pltpu.CompilerParams(
            dimension_semantics=("arbitrary", "arbitrary"), vmem_limit_bytes=VMEM_LIMIT),
        name="ffn_up",
    )(xb, w_gu, w_gu, *cast)


def _ffn_down_ln_kernel(h_ref, wd_ref, x_ref, g_ref, b_ref, *rest, alpha, layer):
    n_cast = (len(rest) - 2) // 2
    o_ref, ob_ref = rest[n_cast], rest[n_cast + 1]
    for src, dst in zip(rest[:n_cast], rest[n_cast + 2:]):
        dst[...] = src[...].astype(dst.dtype)
    for r0 in range(0, x_ref.shape[0], FFN_DOWN_SUB):
        rows = slice(r0, r0 + FFN_DOWN_SUB)
        y = alpha * x_ref[rows, :] + 0.5 * jnp.dot(h_ref[rows, :], wd_ref[...], preferred_element_type=F32)
        out = _layer_norm(y, _param_row(g_ref, layer), _param_row(b_ref, layer))
        o_ref[rows, :] = out
        ob_ref[rows, :] = out.astype(ob_ref.dtype)


def _ffn_down_ln(h, w_d, x, ln_g, ln_b, layer, alpha, cast=()):
    s, d = x.shape
    d_ff = h.shape[1]
    tm = FFN_DOWN_TM
    assert s % tm == 0 and tm % FFN_DOWN_SUB == 0
    c_in, c_out, c_shapes = _cast_specs(cast, layer, s // tm, lambda i: i)
    return pl.pallas_call(
        functools.partial(_ffn_down_ln_kernel, alpha=alpha, layer=layer),
        grid=(s // tm,),
        in_specs=[
            pl.BlockSpec((tm, d_ff), lambda i: (i, 0)),
            pl.BlockSpec((d_ff, d), lambda i: (0, 0), pipeline_mode=pl.Buffered(1)),
            pl.BlockSpec((tm, d), lambda i: (i, 0)),
            _param_spec(ln_g),
            _param_spec(ln_b),
        ] + c_in,
        out_specs=[pl.BlockSpec((tm, d), lambda i: (i, 0)), pl.BlockSpec((tm, d), lambda i: (i, 0))] + c_out,
        out_shape=[jax.ShapeDtypeStruct((s, d), F32), jax.ShapeDtypeStruct((s, d), BF16)] + c_shapes,
        compiler_params=pltpu.CompilerParams(
            dimension_semantics=("arbitrary",), vmem_limit_bytes=VMEM_LIMIT),
        name="ffn_down_ln",
    )(h, w_d, x, ln_g, ln_b, *cast)


def _rope(t, cos, sin_lo, sin_hi):
    return t * cos + pltpu.roll(t, 96, 1) * sin_lo + pltpu.roll(t, 32, 1) * sin_hi


def _in_proj_kernel(xb_ref, w_ref, cos_ref, slo_ref, shi_ref, qg_ref, kg_ref,
                    qnt_ref, kna_ref, vnt_ref, qt_ref, kgo_ref, vt_ref, *, d_na, layer):
    tn = PROJ_TN
    heads = tn // HEAD_DIM
    tm = xb_ref.shape[0]

    def cols(c0):
        return jnp.dot(xb_ref[...], w_ref[:, c0:c0 + tn], preferred_element_type=F32)

    def head(acc, h):
        return acc[:, h * HEAD_DIM:(h + 1) * HEAD_DIM]

    def normed_rope(t, gain):
        t = t * _rms_scale(t) * gain
        return _rope(t, cos_ref[...], slo_ref[...], shi_ref[...])

    for c in range(0, d_na, tn):
        acc = cols(3 * d_na + c)
        for h in range(heads):
            t = normed_rope(head(acc, h), _param_row(qg_ref, layer)) * SM_SCALE
            qt_ref[c + h * HEAD_DIM:c + (h + 1) * HEAD_DIM, :] = t.T.astype(qt_ref.dtype)
    acc = cols(4 * d_na)
    for h in range(N_KV_HEADS):
        kgo_ref[:, h * HEAD_DIM:(h + 1) * HEAD_DIM] = normed_rope(head(acc, h), _param_row(kg_ref, layer)).astype(kgo_ref.dtype)
    acc = cols(4 * d_na + N_KV_HEADS * HEAD_DIM)
    for h in range(N_KV_HEADS):
        vt_ref[h, 0] = head(acc, h).T.astype(vt_ref.dtype)
    for c in range(0, d_na, tn):
        acc = cols(c) * SM_SCALE
        for h in range(heads):
            qnt_ref[c + h * HEAD_DIM:c + (h + 1) * HEAD_DIM, :] = head(acc, h).T.astype(qnt_ref.dtype)
    for c in range(0, d_na, tn):
        acc = cols(2 * d_na + c)
        for h in range(heads):
            vh = head(acc, h).T.astype(vnt_ref.dtype)
            for t0 in range(0, tm, NA_TOK):
                vnt_ref[c // HEAD_DIM + h, t0 // NA_TOK] = vh[:, t0:t0 + NA_TOK]
    for c in range(0, d_na, tn):
        kna_ref[:, c:c + tn] = cols(d_na + c).astype(kna_ref.dtype)


def _in_proj(xb, w_in, cos, sin_lo, sin_hi, q_gain, k_gain, layer):
    s, d = xb.shape
    d_in = w_in.shape[1]
    tm, tn = PROJ_TM, PROJ_TN
    d_na = d // 2
    n_heads_na = d_na // HEAD_DIM
    d_kv = N_KV_HEADS * HEAD_DIM
    assert d_na % tn == 0 and d_kv == tn and d_in == 4 * d_na + 2 * d_kv and tm == GQA_TK and tm % NA_TOK == 0
    return pl.pallas_call(
        functools.partial(_in_proj_kernel, d_na=d_na, layer=layer),
        grid=(s // tm,),
        in_specs=[
            pl.BlockSpec((tm, d), lambda i: (i, 0)),
            pl.BlockSpec((d, d_in), lambda i: (0, 0), pipeline_mode=pl.Buffered(1)),
            pl.BlockSpec((tm, HEAD_DIM), lambda i: (i, 0)),
            pl.BlockSpec((tm, HEAD_DIM), lambda i: (i, 0)),
            pl.BlockSpec((tm, HEAD_DIM), lambda i: (i, 0)),
            _param_spec(q_gain),
            _param_spec(k_gain),
        ],
        out_specs=[
            pl.BlockSpec((d_na, tm), lambda i: (0, i)),
            pl.BlockSpec((tm, d_na), lambda i: (i, 0)),
            pl.BlockSpec((n_heads_na, tm // NA_TOK, HEAD_DIM, NA_TOK), lambda i: (0, i, 0, 0)),
            pl.BlockSpec((d_na, tm), lambda i: (0, i)),
            pl.BlockSpec((tm, d_kv), lambda i: (i, 0)),
            pl.BlockSpec((N_KV_HEADS, 1, HEAD_DIM, tm), lambda i: (0, i, 0, 0)),
        ],
        out_shape=[
            jax.ShapeDtypeStruct((d_na, s), BF16),
            jax.ShapeDtypeStruct((s, d_na), BF16),
            jax.ShapeDtypeStruct((n_heads_na, s // NA_TOK, HEAD_DIM, NA_TOK), BF16),
            jax.ShapeDtypeStruct((d_na, s), BF16),
            jax.ShapeDtypeStruct((s, d_kv), BF16),
            jax.ShapeDtypeStruct((N_KV_HEADS, s // tm, HEAD_DIM, tm), BF16),
        ],
        compiler_params=pltpu.CompilerParams(
            dimension_semantics=("parallel",), vmem_limit_bytes=VMEM_LIMIT),
        name="in_proj",
    )(xb, w_in, cos, sin_lo, sin_hi, q_gain, k_gain)


def _na_kernel(idx_ref, qt_ref, k_ref, vt_ref, bias_ref, o_ref, *, blocks_per_step, n_rows):
    step = pl.program_id(1)
    n_blocks = n_rows // NA_G
    tq = NA_G * GRID_W
    slab = NA_SLAB * GRID_W
    pairs = NA_G // 2

    def scores(i):
        b = step * blocks_per_step + i
        start_row = jnp.clip(b * NA_G - NA_KH // 2, 0, n_rows - NA_SLAB)
        start = pl.multiple_of(start_row * GRID_W, NA_TOK)
        k = k_ref[pl.ds(start, slab), :]
        st = jnp.dot(k, qt_ref[:, i * tq:(i + 1) * tq], preferred_element_type=F32)
        pattern = jnp.where(b == 0, 0, jnp.where(b == n_blocks - 1, 2, 1))
        rows = []
        for u in range(NA_SLAB):
            tiles = [st[u * GRID_W:(u + 1) * GRID_W, j * 2 * GRID_W:(j + 1) * 2 * GRID_W]
                     + bias_ref[idx_ref[(pattern * NA_SLAB + u) * pairs + j]] for j in range(pairs)]
            rows.append(jnp.concatenate(tiles, axis=1))
        return jnp.concatenate(rows, axis=0), start_row // (NA_TOK // GRID_W)

    ahead = [scores(i) for i in range(min(NA_AHEAD, blocks_per_step))]
    for i in range(blocks_per_step):
        st, tok_blk = ahead.pop(0)
        if i + NA_AHEAD < blocks_per_step:
            ahead.append(scores(i + NA_AHEAD))
        m = jnp.max(_fold_row_groups(st, jnp.maximum), axis=0, keepdims=True)
        p = jnp.exp2(st - m)
        l = jnp.sum(_fold_row_groups(p, jnp.add), axis=0, keepdims=True)
        vt = jnp.concatenate([vt_ref[tok_blk + t] for t in range(slab // NA_TOK)], axis=1)
        ot = jnp.dot(vt, p.astype(BF16), preferred_element_type=F32) / l
        o_ref[i * tq:(i + 1) * tq, :] = ot.T.astype(o_ref.dtype)


def _na_attn(qnt, kna, vnt, bias, bias_idx, layer):
    d_na, s = qnt.shape
    n_heads = d_na // HEAD_DIM
    n_rows = s // GRID_W
    blocks_per_step = NA_BLOCKS
    rows_per_step = blocks_per_step * NA_G
    tok_rows = NA_TOK // GRID_W
    assert n_rows % rows_per_step == 0 and n_rows >= NA_SLAB and NA_G % 2 == 0
    assert NA_G % tok_rows == 0 and (NA_KH // 2) % tok_rows == 0 and NA_SLAB % tok_rows == 0
    tq = rows_per_step * GRID_W
    n_tiles = bias.shape[2]
    return pl.pallas_call(
        functools.partial(_na_kernel, blocks_per_step=blocks_per_step, n_rows=n_rows),
        grid=(n_heads, n_rows // rows_per_step),
        in_specs=[
            pl.BlockSpec(memory_space=pltpu.SMEM),
            pl.BlockSpec((HEAD_DIM, tq), lambda hd, rb: (hd, rb)),
            pl.BlockSpec((s, HEAD_DIM), lambda hd, rb: (0, hd)),
            pl.BlockSpec((None, s // NA_TOK, HEAD_DIM, NA_TOK), lambda hd, rb: (hd, 0, 0, 0)),
            pl.BlockSpec((None, None, n_tiles, GRID_W, 2 * GRID_W), lambda hd, rb: (layer, hd, 0, 0, 0)),
        ],
        out_specs=pl.BlockSpec((tq, HEAD_DIM), lambda hd, rb: (rb, hd)),
        out_shape=jax.ShapeDtypeStruct((s, d_na), BF16),
        compiler_params=pltpu.CompilerParams(
            dimension_semantics=("parallel", "arbitrary"), vmem_limit_bytes=VMEM_LIMIT),
        name="na_attn",
    )(bias_idx, qnt, kna, vnt, bias)


def _na_bias_tiles(rel_bias):
    w, kw = GRID_W, NA_KW
    nl, nh, nd, ne = rel_bias.shape
    c = np.arange(w)
    dcol = np.clip(c[:, None] - c[None, :], -(kw - 1), kw - 1) + (kw - 1)
    onehot = (dcol.reshape(-1)[None, :] == np.arange(ne)[:, None]).astype(np.float32)
    toep = jnp.dot(rel_bias.astype(F32).reshape(nl * nh * nd, ne) * LOG2E, jnp.asarray(onehot),
                   precision=lax.Precision.HIGHEST).reshape(nl, nh, nd, w, w)
    c0 = np.clip(c - kw // 2, 0, w - kw)
    col_in = (c[:, None] >= c0[None, :]) & (c[:, None] < c0[None, :] + kw)
    toep = jnp.where(jnp.asarray(col_in), toep, MASK_VALUE)
    masked = jnp.full((nl, nh, 1, w, w), MASK_VALUE, F32)
    ext = jnp.concatenate([masked, toep, masked], axis=2)
    first, second = ext[:, :, 1:], ext[:, :, :-1]
    blank = jnp.broadcast_to(masked, first.shape)
    return jnp.concatenate([jnp.concatenate([first, second], axis=-1),
                            jnp.concatenate([blank, second], axis=-1),
                            jnp.concatenate([first, blank], axis=-1)], axis=2)


def _na_bias_index(n_rows):
    n_var = 2 * NA_KH
    first, interior, last = (0, 0), (1, -(NA_KH // 2)), (n_rows // NA_G - 1, NA_G - NA_SLAB)
    idx = []
    for b, rel in (first, interior, last):
        lo = [min(max(b * NA_G + rr - NA_KH // 2, 0), n_rows - NA_KH) - (b * NA_G + rel) for rr in range(NA_G)]
        for u in range(NA_SLAB):
            for rr in range(0, NA_G, 2):
                has = [lo[r] <= u < lo[r] + NA_KH for r in (rr, rr + 1)]
                f = u + rel - rr + NA_KH - 1
                if has[0] and has[1]:
                    idx.append(f)
                elif has[1]:
                    idx.append(n_var + f)
                elif has[0]:
                    idx.append(2 * n_var + f)
                else:
                    idx.append(3 * n_var - 1)
    return jnp.asarray(idx, jnp.int32)


def _gqa_kernel(qt_ref, k_ref, vt_ref, o_ref, sa_ref, sb_ref, *, tk, nk):
    tq = qt_ref.shape[1]
    qt = qt_ref[...]

    qblocks = [slice(n, n + GQA_QB) for n in range(0, tq, GQA_QB)]

    def key_chunk(c):
        return k_ref[pl.ds(pl.multiple_of(c * tk, tk), tk), :]

    def scores_block(k, qb, dst_ref):
        st = jnp.dot(k, qt[:, qb], preferred_element_type=F32)
        dst_ref[:, qb] = st
        return _fold_row_groups(st, jnp.maximum)

    def scores(c, dst_ref):
        k = key_chunk(c)
        return jnp.concatenate([scores_block(k, qb, dst_ref) for qb in qblocks], axis=1)

    def step(c, src_ref, dst_ref, m8, m, l8, acc):
        m_new = jnp.maximum(m, jnp.max(m8, axis=0, keepdims=True))
        corr = jnp.exp2(m - m_new)
        vt = vt_ref[c]
        k_next = None if dst_ref is None else key_chunk(c + 1)
        m8_next, l8_new, acc_new = [], [], []
        for qb in qblocks:
            if dst_ref is not None:
                m8_next.append(scores_block(k_next, qb, dst_ref))
            p = jnp.exp2(src_ref[:, qb] - m_new[:, qb])
            l8_new.append(corr[:, qb] * l8[:, qb] + _fold_row_groups(p, jnp.add))
            acc_new.append(corr[:, qb] * acc[:, qb]
                           + jnp.dot(vt, p.astype(BF16), preferred_element_type=F32))
        m8 = jnp.concatenate(m8_next, axis=1) if m8_next else m8
        return m8, m_new, jnp.concatenate(l8_new, axis=1), jnp.concatenate(acc_new, axis=1)

    def group(c0, m8, m, l8, acc, last):
        bufs = (sa_ref, sb_ref)
        for t in range(GQA_UNROLL):
            more = t + 1 < GQA_UNROLL or not last
            m8, m, l8, acc = step(c0 + t, bufs[t % 2], bufs[(t + 1) % 2] if more else None, m8, m, l8, acc)
        return m8, m, l8, acc

    init = (scores(0, sa_ref), jnp.full((1, tq), -jnp.inf, F32), jnp.zeros((8, tq), F32),
            jnp.zeros((HEAD_DIM, tq), F32))
    carry = lax.fori_loop(0, nk // GQA_UNROLL - 1,
                          lambda j, cr: group(GQA_UNROLL * j, *cr, last=False), init)
    _, _, l8, acc = group(nk - GQA_UNROLL, *carry, last=True)
    ot = acc / jnp.sum(l8, axis=0, keepdims=True)
    o_ref[...] = ot.T.astype(o_ref.dtype)


def _gqa_attn(qt, kg, vt):
    d_gqa, s = qt.shape
    n_heads = d_gqa // HEAD_DIM
    group = n_heads // N_KV_HEADS
    tq, tk = GQA_TQ, GQA_TK
    assert s % tq == 0 and s % (GQA_UNROLL * tk) == 0 and GQA_UNROLL % 2 == 0
    nk = s // tk
    return pl.pallas_call(
        functools.partial(_gqa_kernel, tk=tk, nk=nk),
        grid=(n_heads, s // tq),
        in_specs=[
            pl.BlockSpec((HEAD_DIM, tq), lambda hd, i: (hd, i)),
            pl.BlockSpec((s, HEAD_DIM), lambda hd, i: (0, hd // group)),
            pl.BlockSpec((None, nk, HEAD_DIM, tk), lambda hd, i: (hd // group, 0, 0, 0)),
        ],
        out_specs=pl.BlockSpec((tq, HEAD_DIM), lambda hd, i: (i, hd)),
        out_shape=jax.ShapeDtypeStruct((s, d_gqa), BF16),
        scratch_shapes=[pltpu.VMEM((tk, tq), F32), pltpu.VMEM((tk, tq), F32)],
        compiler_params=pltpu.CompilerParams(
            dimension_semantics=("parallel", "arbitrary"), vmem_limit_bytes=VMEM_LIMIT),
        name="gqa_attn",
    )(qt, kg, vt)


def _out_proj_ln_kernel(ona_ref, og_ref, x_ref, w_ref, gna_ref, gg_ref, g_ref, b_ref, o_ref, ob_ref, *, alpha, layer):
    for r0 in range(0, ona_ref.shape[0], PROJ_SUB):
        rows = slice(r0, r0 + PROJ_SUB)
        a = ona_ref[rows, :].astype(F32)
        b = og_ref[rows, :].astype(F32)
        mixed_in = jnp.concatenate([(a * _rms_scale(a) * _param_row(gna_ref, layer)).astype(BF16),
                                    (b * _rms_scale(b) * _param_row(gg_ref, layer)).astype(BF16)], axis=1)
        mix = jnp.dot(mixed_in, w_ref[...], preferred_element_type=F32)
        out = _layer_norm(alpha * x_ref[rows, :] + mix, _param_row(g_ref, layer), _param_row(b_ref, layer))
        o_ref[rows, :] = out
        ob_ref[rows, :] = out.astype(ob_ref.dtype)


def _out_proj_ln(o_na, o_g, x, w_out, gn_na, gn_g, ln_g, ln_b, layer, alpha):
    s, d = x.shape
    d_na = o_na.shape[1]
    d_g = o_g.shape[1]
    tm = PROJ_TM
    return pl.pallas_call(
        functools.partial(_out_proj_ln_kernel, alpha=alpha, layer=layer),
        grid=(s // tm,),
        in_specs=[
            pl.BlockSpec((tm, d_na), lambda i: (i, 0)),
            pl.BlockSpec((tm, d_g), lambda i: (i, 0)),
            pl.BlockSpec((tm, d), lambda i: (i, 0)),
            pl.BlockSpec((d_na + d_g, d), lambda i: (0, 0)),
            _param_spec(gn_na),
            _param_spec(gn_g),
            _param_spec(ln_g),
            _param_spec(ln_b),
        ],
        out_specs=[pl.BlockSpec((tm, d), lambda i: (i, 0)), pl.BlockSpec((tm, d), lambda i: (i, 0))],
        out_shape=[jax.ShapeDtypeStruct((s, d), F32), jax.ShapeDtypeStruct((s, d), BF16)],
        compiler_params=pltpu.CompilerParams(
            dimension_semantics=("parallel",), vmem_limit_bytes=VMEM_LIMIT),
        name="out_proj_ln",
    )(o_na, o_g, x, w_out, gn_na, gn_g, ln_g, ln_b)


def _rope_tables(s):
    half = HEAD_DIM // 2
    nfreq = half // 2
    t = np.arange(s)
    row = (t // GRID_W).astype(np.float32)
    col = (t % GRID_W).astype(np.float32)
    inv_freq = (1.0 / (np.float32(ROPE_THETA) ** (np.arange(nfreq, dtype=np.float32) / np.float32(nfreq)))
                ).astype(np.float32)
    lane = np.arange(HEAD_DIM)
    pos = np.where(lane[None, :] < half, row[:, None], col[:, None])
    ang = (pos * inv_freq[lane % nfreq][None, :]).astype(np.float32)
    cos, sin = np.cos(ang), np.sin(ang)
    lower = (lane % half) < nfreq
    sin_lo = np.where(lower[None, :], -sin, np.float32(0.0))
    sin_hi = np.where(lower[None, :], np.float32(0.0), sin)
    return tuple(jnp.asarray(a, F32) for a in (cos, sin_lo, sin_hi))


def kernel(x, ffn1_w_gate_up, ffn1_w_down, ln1_g, ln1_b, w_in, na_rel_bias, q_norm_g, k_norm_g,
           gn_na_g, gn_gqa_g, w_out, ln2_g, ln2_b, ffn2_w_gate_up, ffn2_w_down, ln3_g, ln3_b):
    batch, s, d = x.shape
    depth = w_in.shape[0]
    alpha = (2.0 * depth) ** 0.25

    cos, sin_lo, sin_hi = _rope_tables(s)
    bias = _na_bias_tiles(na_rel_bias)
    bias_idx = _na_bias_index(s // GRID_W)
    ln1g, ln1b, ln2g, ln2b, ln3g, ln3b = ln1_g, ln1_b, ln2_g, ln2_b, ln3_g, ln3_b
    qg, kg, gna, ggq = q_norm_g, k_norm_g, gn_na_g, gn_gqa_g

    outs = []
    for bi in range(batch):
        xs = x[bi]
        xb = xs.astype(BF16)
        for layer in range(depth):
            hid, wd_b = _ffn_up(xb, ffn1_w_gate_up, layer, cast=(ffn1_w_down,))
            xs, xb, w_in_b, w_out_b = _ffn_down_ln(hid, wd_b, xs, ln1g, ln1b, layer, alpha, cast=(w_in, w_out))
            qn_t, k_n, vn_t, q_t, k_g, v_t = _in_proj(xb, w_in_b, cos, sin_lo, sin_hi, qg, kg, layer)
            o_na = _na_attn(qn_t, k_n, vn_t, bias, bias_idx, layer)
            o_g = _gqa_attn(q_t, k_g, v_t)
            xs, xb = _out_proj_ln(o_na, o_g, xs, w_out_b, gna, ggq, ln2g, ln2b, layer, alpha)
            hid, wd_b = _ffn_up(xb, ffn2_w_gate_up, layer, cast=(ffn2_w_down,))
            xs, xb = _ffn_down_ln(hid, wd_b, xs, ln3g, ln3b, layer, alpha)
        outs.append(xs)
    return outs[0][None] if batch == 1 else jnp.stack(outs, axis=0)
```
